```python
import math
import jax, jax.numpy as jnp
from jax import lax
import numpy as np

D_MODEL = 2048
BATCH = 1
SEQ = 8192
DEPTH = 4

GRID_W = 64
CTX_LEN = 256
HEAD_DIM = 128
N_HEADS_TOTAL = D_MODEL // HEAD_DIM
A_HEADS = N_HEADS_TOTAL // 2
A_KV_HEADS = A_HEADS // 4
B_HEADS = N_HEADS_TOTAL // 4
C_HEADS = N_HEADS_TOTAL // 4
C_HALF = HEAD_DIM // 2
NA_KH_MAX = 8
NA_KW = 16
D_FF = int(math.ceil(8 * D_MODEL / 3 / 128)) * 128
CONV_W = 3
Q_BLOCK = 128
ROPE_THETA = 10000.0
EPS = 1e-6
N_BRANCH = 3

A_Q = A_HEADS * HEAD_DIM
A_KV = A_KV_HEADS * HEAD_DIM
B_W = B_HEADS * HEAD_DIM
C_QK = C_HEADS * 2 * C_HALF
C_V = C_HEADS * HEAD_DIM
PART_SIZES = (A_Q, A_KV, A_KV, B_W, B_W, B_W, C_QK, C_QK, C_V)
SPLITS = tuple(int(v) for v in np.cumsum(PART_SIZES))
D_IN = SPLITS[-1] + N_BRANCH * D_MODEL

kernel_name = "hybrid_gqa_natten_diffattn_convffn_dit"


def rms_norm(x, g):
    xf = x.astype(jnp.float32)
    y = xf * lax.rsqrt(jnp.mean(xf * xf, axis=-1, keepdims=True) + EPS)
    return (y * g.astype(jnp.float32)).astype(x.dtype)


def modulate(x, shift, scale):
    return x * (1 + scale) + shift


def axial_rope_tables(n_tokens, dim):
    n_freq = dim // 4
    inv_freq = ROPE_THETA ** (-jnp.arange(n_freq, dtype=jnp.float32) / n_freq)
    t = jnp.arange(n_tokens, dtype=jnp.int32)
    pos = jnp.stack([t // GRID_W, t % GRID_W], axis=-1).astype(jnp.float32)
    ang = (pos[:, :, None] * inv_freq).reshape(n_tokens, 2 * n_freq)
    return jnp.cos(ang), jnp.sin(ang)


def apply_axial_rope(x, cos, sin):
    S, dim = x.shape[1], x.shape[-1]
    nf = dim // 4
    xr = x.astype(jnp.float32).reshape(x.shape[:-1] + (2, 2, nf))
    bshape = (S,) + (1,) * (x.ndim - 3) + (2, 1, nf)
    c = cos.reshape(bshape)
    s = sin.reshape(bshape)
    x1, x2 = xr[..., 0:1, :], xr[..., 1:2, :]
    out = jnp.concatenate([x1 * c - x2 * s, x2 * c + x1 * s], axis=-2)
    return out.reshape(x.shape).astype(x.dtype)


def project_heads(u, w_in, gq_a, gk_a, gq_b, gk_b, gq_c, gk_c, rope_a, rope_c):
    B, T, _ = u.shape
    parts = jnp.split(u @ w_in, SPLITS, axis=-1)
    qa = rms_norm(parts[0].reshape(B, T, A_HEADS, HEAD_DIM), gq_a)
    ka = rms_norm(parts[1].reshape(B, T, A_KV_HEADS, HEAD_DIM), gk_a)
    va = parts[2].reshape(B, T, A_KV_HEADS, HEAD_DIM)
    qb = rms_norm(parts[3].reshape(B, T, B_HEADS, HEAD_DIM), gq_b)
    kb = rms_norm(parts[4].reshape(B, T, B_HEADS, HEAD_DIM), gk_b)
    vb = parts[5].reshape(B, T, B_HEADS, HEAD_DIM)
    qc = rms_norm(parts[6].reshape(B, T, C_HEADS, 2, C_HALF), gq_c)
    kc = rms_norm(parts[7].reshape(B, T, C_HEADS, 2, C_HALF), gk_c)
    vc = parts[8].reshape(B, T, C_HEADS, HEAD_DIM)
    gates = jax.nn.sigmoid(parts[9].astype(jnp.float32)).astype(u.dtype).reshape(B, T, N_BRANCH, D_MODEL)
    if rope_a is not None:
        qa = apply_axial_rope(qa, *rope_a)
        ka = apply_axial_rope(ka, *rope_a)
        qc = apply_axial_rope(qc, *rope_c)
        kc = apply_axial_rope(kc, *rope_c)
    return qa, ka, va, qb, kb, vb, qc, kc, vc, gates


def sweep_query_blocks(attend, q):
    B, S = q.shape[:2]
    nb = S // Q_BLOCK
    qb = jnp.moveaxis(q.reshape((B, nb, Q_BLOCK) + q.shape[2:]), 1, 0)
    o = lax.map(attend, qb)
    return jnp.moveaxis(o, 0, 1).reshape((B, S) + o.shape[3:])


def gqa_attend(q, k, v):
    B, Q, H, dh = q.shape
    hkv = k.shape[2]
    qg = q.reshape(B, Q, hkv, H // hkv, dh)
    s = jnp.einsum('bqkgd,btkd->bkgqt', qg, k).astype(jnp.float32) * (dh ** -0.5)
    p = jax.nn.softmax(s, axis=-1).astype(v.dtype)
    o = jnp.einsum('bkgqt,btkd->bqkgd', p, v)
    return o.reshape(B, Q, H * dh)


def diff_attend(q, k, v, lam):
    dc = q.shape[-1]
    s = jnp.einsum('bqhcd,bthcd->bhcqt', q, k).astype(jnp.float32) * (dc ** -0.5)
    p = jax.nn.softmax(s, axis=-1)
    a = (p[:, :, 0] - lam * p[:, :, 1]).astype(v.dtype)
    return jnp.einsum('bhqt,bthd->bqhd', a, v)


def neighbourhood_attention(q, k, v, k_ctx, v_ctx, rpb):
    B, S, H, dh = q.shape
    rows = S // GRID_W
    kh = min(NA_KH_MAX, rows)
    kw = NA_KW
    scale = dh ** -0.5
    col = jnp.arange(GRID_W)
    col_start = jnp.clip(col - kw // 2, 0, GRID_W - kw)
    col_idx = col_start[:, None] + jnp.arange(kw)[None, :]
    dc = col_idx - col[:, None] + (NA_KW - 1)
    qg = jnp.moveaxis(q.reshape(B, rows, GRID_W, H, dh), 1, 0)
    kg = k.reshape(B, rows, GRID_W, H, dh)
    vg = v.reshape(B, rows, GRID_W, H, dh)

    def one_row(args):
        r, q_row = args
        rs = jnp.clip(r - kh // 2, 0, rows - kh)
        k_rows = lax.dynamic_slice_in_dim(kg, rs, kh, axis=1)
        v_rows = lax.dynamic_slice_in_dim(vg, rs, kh, axis=1)
        k_win = k_rows[:, :, col_idx]
        v_win = v_rows[:, :, col_idx]
        dr = rs + jnp.arange(kh) - r + (NA_KH_MAX - 1)
        bias = rpb[:, dr[:, None, None], dc[None, :, :]]
        bias = jnp.transpose(bias, (0, 2, 1, 3)).astype(jnp.float32)
        s_win = jnp.einsum('bqhd,bjqwhd->bhqjw', q_row, k_win).astype(jnp.float32) * scale + bias[None]
        s_win = s_win.reshape(B, H, GRID_W, kh * kw)
        s_ctx = jnp.einsum('bqhd,blhd->bhql', q_row, k_ctx).astype(jnp.float32) * scale
        p = jax.nn.softmax(jnp.concatenate([s_win, s_ctx], axis=-1), axis=-1).astype(v.dtype)
        p_win = p[..., :kh * kw].reshape(B, H, GRID_W, kh, kw)
        p_ctx = p[..., kh * kw:]
        return (jnp.einsum('bhqjw,bjqwhd->bqhd', p_win, v_win)
                + jnp.einsum('bhql,blhd->bqhd', p_ctx, v_ctx))

    o = lax.map(one_row, (jnp.arange(rows), qg))
    return jnp.moveaxis(o, 0, 1).reshape(B, S, H * dh)


def merge_branches(oa, ob, oc, gates, w_br_a, w_br_b, w_br_c, w_o):
    merged = (gates[:, :, 0] * (oa @ w_br_a)
              + gates[:, :, 1] * (ob @ w_br_b)
              + gates[:, :, 2] * (oc @ w_br_c))
    return merged @ w_o


def conv_ffn(u, w_up, conv_w, conv_b, w_down):
    T = u.shape[1]
    h = u @ w_up
    pad = CONV_W // 2
    hp = jnp.pad(h, ((0, 0), (pad, pad), (0, 0)))
    h = conv_b + sum(hp[:, j:j + T] * conv_w[j] for j in range(CONV_W))
    a, g = jnp.split(h, 2, axis=-1)
    return (jax.nn.silu(g) * a) @ w_down


def setup_inputs(seed: int = 0) -> dict:
    key = jax.random.key(seed)
    ks = jax.random.split(key, 32)

    def nrm(k, shape, scale):
        return jax.random.normal(k, shape, jnp.float32) * scale

    D = D_MODEL
    return {
        "x": nrm(ks[0], (BATCH, SEQ, D), 1.0),
        "c": nrm(ks[1], (BATCH, D), 1.0),
        "ctx": nrm(ks[2], (BATCH, CTX_LEN, D), 1.0),
        "c_ctx": nrm(ks[3], (D,), 1.0),
        "w_mod": nrm(ks[4], (DEPTH, D, 6 * D), 0.5 * D ** -0.5),
        "b_mod": nrm(ks[5], (DEPTH, 6 * D), 0.02),
        "g_norm1": 1.0 + nrm(ks[6], (DEPTH, D), 0.05),
        "w_in": nrm(ks[7], (DEPTH, D, D_IN), D ** -0.5),
        "gq_a": 1.0 + nrm(ks[8], (DEPTH, HEAD_DIM), 0.05),
        "gk_a": 1.0 + nrm(ks[9], (DEPTH, HEAD_DIM), 0.05),
        "gq_b": 1.0 + nrm(ks[10], (DEPTH, HEAD_DIM), 0.05),
        "gk_b": 1.0 + nrm(ks[11], (DEPTH, HEAD_DIM), 0.05),
        "rpb_b": nrm(ks[12], (DEPTH, B_HEADS, 2 * NA_KH_MAX - 1, 2 * NA_KW - 1), 0.5),
        "gq_c": 1.0 + nrm(ks[13], (DEPTH, C_HALF), 0.05),
        "gk_c": 1.0 + nrm(ks[14], (DEPTH, C_HALF), 0.05),
        "lam_q1": nrm(ks[15], (DEPTH, C_HALF), 0.1),
        "lam_k1": nrm(ks[16], (DEPTH, C_HALF), 0.1),
        "lam_q2": nrm(ks[17], (DEPTH, C_HALF), 0.1),
        "lam_k2": nrm(ks[18], (DEPTH, C_HALF), 0.1),
        "g_subln_c": 1.0 + nrm(ks[19], (DEPTH, HEAD_DIM), 0.05),
        "w_br_a": nrm(ks[20], (DEPTH, A_Q, D), A_Q ** -0.5),
        "w_br_b": nrm(ks[21], (DEPTH, B_W, D), B_W ** -0.5),
        "w_br_c": nrm(ks[22], (DEPTH, C_V, D), C_V ** -0.5),
        "w_o": nrm(ks[23], (DEPTH, D, D), D ** -0.5),
        "g_norm2": 1.0 + nrm(ks[24], (DEPTH, D), 0.05),
        "w_up": nrm(ks[25], (DEPTH, D, 2 * D_FF), D ** -0.5),
        "conv_w": nrm(ks[26], (DEPTH, CONV_W, 2 * D_FF), CONV_W ** -0.5),
        "conv_b": nrm(ks[27], (DEPTH, 2 * D_FF), 0.02),
        "w_down": nrm(ks[28], (DEPTH, D_FF, D), D_FF ** -0.5),
    }


def reference(x, c, ctx, c_ctx, w_mod, b_mod, g_norm1, w_in, gq_a, gk_a, gq_b, gk_b, rpb_b,
              gq_c, gk_c, lam_q1, lam_k1, lam_q2, lam_k2, g_subln_c, w_br_a, w_br_b, w_br_c,
              w_o, g_norm2, w_up, conv_w, conv_b, w_down):
    B, S, D = x.shape
    rope_a = axial_rope_tables(S, HEAD_DIM)
    rope_c = axial_rope_tables(S, C_HALF)
    hx, hc = x, ctx
    for l in range(DEPTH):
        last = l == DEPTH - 1
        mod_x = (jax.nn.silu(c) @ w_mod[l] + b_mod[l]).reshape(B, 1, 6, D)
        mod_c = (jax.nn.silu(c_ctx) @ w_mod[l] + b_mod[l]).reshape(6, D)
        lam_init = 0.8 - 0.6 * math.exp(-0.3 * l)
        lam = (jnp.exp(jnp.sum(lam_q1[l].astype(jnp.float32) * lam_k1[l].astype(jnp.float32)))
               - jnp.exp(jnp.sum(lam_q2[l].astype(jnp.float32) * lam_k2[l].astype(jnp.float32)))
               + lam_init)

        ux = modulate(rms_norm(hx, g_norm1[l]), mod_x[:, :, 0], mod_x[:, :, 1])
        uc = modulate(rms_norm(hc, g_norm1[l]), mod_c[0], mod_c[1])
        qa, ka, va, qb, kb, vb, qc, kc, vc, gx = project_heads(
            ux, w_in[l], gq_a[l], gk_a[l], gq_b[l], gk_b[l], gq_c[l], gk_c[l], rope_a, rope_c)
        qa_t, ka_t, va_t, qb_t, kb_t, vb_t, qc_t, kc_t, vc_t, gt = project_heads(
            uc, w_in[l], gq_a[l], gk_a[l], gq_b[l], gk_b[l], gq_c[l], gk_c[l], None, None)

        k_all_a = jnp.concatenate([ka, ka_t], axis=1)
        v_all_a = jnp.concatenate([va, va_t], axis=1)
        oa = sweep_query_blocks(lambda qblk: gqa_attend(qblk, k_all_a, v_all_a), qa)
        ob = neighbourhood_attention(qb, kb, vb, kb_t, vb_t, rpb_b[l])
        k_all_c = jnp.concatenate([kc, kc_t], axis=1)
        v_all_c = jnp.concatenate([vc, vc_t], axis=1)
        oc = sweep_query_blocks(lambda qblk: diff_attend(qblk, k_all_c, v_all_c, lam), qc)
        oc = (rms_norm(oc, g_subln_c[l]) * (1 - lam_init)).reshape(B, S, C_V)
        hx = hx + mod_x[:, :, 2] * merge_branches(oa, ob, oc, gx, w_br_a[l], w_br_b[l], w_br_c[l], w_o[l])

        if not last:
            T = hc.shape[1]
            oa_t = gqa_attend(qa_t, ka_t, va_t)
            ob_t = gqa_attend(qb_t, kb_t, vb_t)
            oc_t = diff_attend(qc_t, kc_t, vc_t, lam)
            oc_t = (rms_norm(oc_t, g_subln_c[l]) * (1 - lam_init)).reshape(B, T, C_V)
            hc = hc + mod_c[2] * merge_branches(oa_t, ob_t, oc_t, gt, w_br_a[l], w_br_b[l], w_br_c[l], w_o[l])

        vx = modulate(rms_norm(hx, g_norm2[l]), mod_x[:, :, 3], mod_x[:, :, 4])
        hx = hx + mod_x[:, :, 5] * conv_ffn(vx, w_up[l], conv_w[l], conv_b[l], w_down[l])
        if not last:
            vt = modulate(rms_norm(hc, g_norm2[l]), mod_c[3], mod_c[4])
            hc = hc + mod_c[5] * conv_ffn(vt, w_up[l], conv_w[l], conv_b[l], w_down[l])
    return hx
```

```python
import functools
import math

import numpy as np
import jax
import jax.numpy as jnp
from jax import lax
from jax.experimental import pallas as pl
from jax.experimental.pallas import tpu as pltpu

F32 = jnp.float32
BF16 = jnp.bfloat16

HEAD = 128
GRID_W = 64
A_HEADS, A_KV = 8, 2
B_HEADS = 4
C_HEADS = 4
C_HALF = 64
NA_KH, NA_KW = 8, 16
ROPE_THETA = 10000.0
EPS = 1e-6
NEG = -1e30

QA0, KA0, VA0 = 0, 8, 10
QB0, KB0, VB0 = 12, 16, 20
QC0, KC0, VC0 = 24, 28, 32
N_QKV = 36 * HEAD

VMEM_LIMIT = 56 * 1024 * 1024
BM = 768
BN = 512
BN_MERGE = 256
HALO = 16
BQ = 256
BK = 1024
NBR_QROWS = BQ // GRID_W
NBR_KROWS = NBR_QROWS + NA_KH


def _cparams(sem):
    return pltpu.CompilerParams(dimension_semantics=sem, vmem_limit_bytes=VMEM_LIMIT)


def _sigmoid(x):
    return 1.0 / (1.0 + jnp.exp(-x))


def _dot(a, b):
    return jnp.dot(a, b, preferred_element_type=F32)


def _dot_nt(a, b):
    return lax.dot_general(a, b, (((1,), (1,)), ((), ())), preferred_element_type=F32)


def _mod_kernel(cc_ref, w_ref, b_ref, o_ref):
    cc = cc_ref[...]
    a = (cc * _sigmoid(cc)).astype(BF16)
    o_ref[...] = _dot(a, w_ref[...].astype(BF16)) + b_ref[...]


def _modulation(cc, w_mod, b_mod):
    depth, d, n = w_mod.shape
    bn = 1024
    return pl.pallas_call(
        _mod_kernel,
        out_shape=jax.ShapeDtypeStruct((depth, 8, n), F32),
        grid=(depth, n // bn),
        in_specs=[
            pl.BlockSpec((8, d), lambda l, j: (0, 0)),
            pl.BlockSpec((None, d, bn), lambda l, j: (l, 0, j)),
            pl.BlockSpec((None, 1, bn), lambda l, j: (l, 0, j)),
        ],
        out_specs=pl.BlockSpec((None, 8, bn), lambda l, j: (l, 0, j)),
        compiler_params=_cparams(("arbitrary", "arbitrary")),
        name="adaln_mod",
    )(cc, w_mod, b_mod.reshape(depth, 1, n))


def _norm_modulate(x, g, mod_ref, col0, row0, n_lat):
    d = x.shape[-1]
    y = x * lax.rsqrt(jnp.mean(x * x, axis=-1, keepdims=True) + EPS) * g
    rows = row0 + lax.broadcasted_iota(jnp.int32, (x.shape[0], 1), 0)
    is_ctx = rows >= n_lat
    shift = jnp.where(is_ctx, mod_ref[1:2, col0:col0 + d], mod_ref[0:1, col0:col0 + d])
    scale = jnp.where(is_ctx, mod_ref[1:2, col0 + d:col0 + 2 * d], mod_ref[0:1, col0 + d:col0 + 2 * d])
    return (y * (1.0 + scale) + shift).astype(BF16)


def _rms_heads(x, gain, group):
    x2 = x * x
    if group == HEAD:
        ms = jnp.mean(x2, axis=-1, keepdims=True)
    else:
        lane = lax.broadcasted_iota(jnp.int32, x.shape, 1)
        lo = lane < group
        s_lo = jnp.sum(jnp.where(lo, x2, 0.0), axis=-1, keepdims=True)
        s_hi = jnp.sum(jnp.where(lo, 0.0, x2), axis=-1, keepdims=True)
        ms = jnp.where(lo, s_lo, s_hi) * (1.0 / group)
    return x * lax.rsqrt(ms + EPS) * gain


def _rope(x, cos, sin_signed, half):
    lane = lax.broadcasted_iota(jnp.int32, x.shape, 1)
    first = (lane % (2 * half)) < half
    partner = jnp.where(first, pltpu.roll(x, HEAD - half, 1), pltpu.roll(x, half, 1))
    return x * cos + partner * sin_signed


_QKV_BLOCKS = (
    (("rope_a", 0),) * 4, (("rope_a", 0),) * 4,
    (("rope_a", 1), ("rope_a", 1), ("raw", 0), ("raw", 0)),
    (("norm", 2),) * 4, (("norm", 3),) * 4, (("raw", 0),) * 4,
    (("rope_c", 4),) * 4, (("rope_c", 5),) * 4, (("raw", 0),) * 4,
)


def _qkv_kernel(h_ref, mod_ref, g_ref, w_ref, gains_ref, ca_ref, sa_ref, cc_ref, sc_ref,
                o_ref, u_ref, *, n_lat):
    i, j = pl.program_id(0), pl.program_id(1)
    bm = h_ref.shape[0]

    @pl.when(j == 0)
    def _():
        u_ref[...] = _norm_modulate(h_ref[...], g_ref[...], mod_ref, 0, i * bm, n_lat)

    acc = _dot(u_ref[...], w_ref[...])

    def epilogue(heads):
        for k, (kind, gr) in enumerate(heads):
            x = acc[:, k * HEAD:(k + 1) * HEAD]
            gain = gains_ref[gr:gr + 1, :]
            if kind == "rope_a":
                x = _rope(_rms_heads(x, gain, HEAD), ca_ref[...], sa_ref[...], HEAD // 4)
            elif kind == "norm":
                x = _rms_heads(x, gain, HEAD)
            elif kind == "rope_c":
                x = _rope(_rms_heads(x, gain, C_HALF), cc_ref[...], sc_ref[...], C_HALF // 4)
            o_ref[:, k * HEAD:(k + 1) * HEAD] = x.astype(BF16)

    for heads in sorted(set(_QKV_BLOCKS)):
        cond = None
        for jj, hh in enumerate(_QKV_BLOCKS):
            if hh == heads:
                cond = (j == jj) if cond is None else (cond | (j == jj))
        pl.when(cond)(functools.partial(epilogue, heads))


def _qkv_proj(h, mod, g, w, gains, rope, n_lat):
    ntok, d = h.shape
    n = w.shape[1]
    ca, sa, cc, sc = rope
    return pl.pallas_call(
        functools.partial(_qkv_kernel, n_lat=n_lat),
        out_shape=jax.ShapeDtypeStruct((ntok, n), BF16),
        grid=(ntok // BM, n // BN),
        in_specs=[
            pl.BlockSpec((BM, d), lambda i, j: (i, 0)),
            pl.BlockSpec(mod.shape, lambda i, j: (0, 0)),
            pl.BlockSpec((1, d), lambda i, j: (0, 0)),
            pl.BlockSpec((d, BN), lambda i, j: (0, j)),
            pl.BlockSpec(gains.shape, lambda i, j: (0, 0)),
            pl.BlockSpec((BM, HEAD), lambda i, j: (i, 0)),
            pl.BlockSpec((BM, HEAD), lambda i, j: (i, 0)),
            pl.BlockSpec((BM, HEAD), lambda i, j: (i, 0)),
            pl.BlockSpec((BM, HEAD), lambda i, j: (i, 0)),
        ],
        out_specs=pl.BlockSpec((BM, BN), lambda i, j: (i, j)),
        scratch_shapes=[pltpu.VMEM((BM, d), BF16)],
        compiler_params=_cparams(("arbitrary", "arbitrary")),
        name="qkv_proj",
    )(h, mod, g, w, gains, ca, sa, cc, sc)


def _flash_kernel(q_ref, k_ref, v_ref, par_ref, o_ref, m_ref, l_ref, acc_ref,
                  *, groups, diff, n_lat, n_ctx, scale):
    i = pl.program_id(1)
    bq = q_ref.shape[0]
    is_lat = i * bq < n_lat

    if diff:
        qv = q_ref[...]
        lane = lax.broadcasted_iota(jnp.int32, qv.shape, 1)
        zero = jnp.zeros_like(qv)
        qs = [jnp.where(lane < C_HALF, qv, zero), jnp.where(lane < C_HALF, zero, qv)]
    else:
        qs = [q_ref[:, g * HEAD:(g + 1) * HEAD] for g in range(groups)]

    m_ref[...] = jnp.full(m_ref.shape, NEG, F32)
    l_ref[...] = jnp.zeros(l_ref.shape, F32)
    acc_ref[...] = jnp.zeros(acc_ref.shape, F32)

    def chunk(k, v):
        for g in range(groups):
            s = _dot_nt(qs[g], k) * scale
            m_prev = m_ref[g]
            m_new = jnp.maximum(m_prev, jnp.max(s, axis=-1, keepdims=True))
            alpha = jnp.exp(m_prev - m_new)
            p = jnp.exp(s - m_new)
            l_ref[g] = alpha * l_ref[g] + jnp.sum(p, axis=-1, keepdims=True)
            acc_ref[g] = alpha * acc_ref[g] + _dot(p.astype(BF16), v)
            m_ref[g] = m_new

    def body(c, carry):
        start = pl.multiple_of(c * BK, BK)
        chunk(k_ref[pl.ds(start, BK), :], v_ref[pl.ds(start, BK), :])
        return carry

    lax.fori_loop(0, jnp.where(is_lat, n_lat // BK, 0), body, 0)
    chunk(k_ref[n_lat:n_lat + n_ctx, :], v_ref[n_lat:n_lat + n_ctx, :])

    if diff:
        lam = (jnp.exp(jnp.sum(par_ref[0:1, :] * par_ref[1:2, :], axis=-1, keepdims=True))
               - jnp.exp(jnp.sum(par_ref[2:3, :] * par_ref[3:4, :], axis=-1, keepdims=True))
               + par_ref[5:6, 0:1])
        o = acc_ref[0] / l_ref[0] - lam * (acc_ref[1] / l_ref[1])
        o = o * lax.rsqrt(jnp.mean(o * o, axis=-1, keepdims=True) + EPS) * par_ref[4:5, :]
        o_ref[...] = (o * (1.0 - par_ref[5:6, 0:1])).astype(BF16)
    else:
        for g in range(groups):
            o_ref[:, g * HEAD:(g + 1) * HEAD] = (acc_ref[g] / l_ref[g]).astype(BF16)


def _flash(qkv, par, *, n_kv_heads, groups, diff, q0, k0, v0, n_lat, n_ctx, scale):
    ntok = qkv.shape[0]
    qw = HEAD if diff else groups * HEAD
    return pl.pallas_call(
        functools.partial(_flash_kernel, groups=groups, diff=diff, n_lat=n_lat, n_ctx=n_ctx, scale=scale),
        out_shape=jax.ShapeDtypeStruct((ntok, n_kv_heads * qw), BF16),
        grid=(n_kv_heads, ntok // BQ),
        in_specs=[
            pl.BlockSpec((BQ, qw), lambda h, i: (i, q0 * HEAD // qw + h)),
            pl.BlockSpec((ntok, HEAD), lambda h, i: (0, k0 + h)),
            pl.BlockSpec((ntok, HEAD), lambda h, i: (0, v0 + h)),
            pl.BlockSpec(par.shape, lambda h, i: (0, 0)),
        ],
        out_specs=pl.BlockSpec((BQ, qw), lambda h, i: (i, h)),
        scratch_shapes=[
            pltpu.VMEM((groups, BQ, 1), F32),
            pltpu.VMEM((groups, BQ, 1), F32),
            pltpu.VMEM((groups, BQ, HEAD), F32),
        ],
        compiler_params=_cparams(("arbitrary", "arbitrary")),
        name="flash_diff" if diff else "flash_gqa",
    )(qkv, qkv, qkv, par)


def _rpb_blocks_kernel(rpb_ref, o_ref):
    n_dr, n_dc = 2 * NA_KH - 1, 2 * NA_KW - 1
    base = pl.program_id(0) * (n_dr * n_dc)
    qc = lax.broadcasted_iota(jnp.int32, (GRID_W, GRID_W), 0)
    kc = lax.broadcasted_iota(jnp.int32, (GRID_W, GRID_W), 1)
    cs = jnp.clip(qc - NA_KW // 2, 0, GRID_W - NA_KW)
    in_win = (kc >= cs) & (kc < cs + NA_KW)
    dc = kc - qc + (NA_KW - 1)
    for a in range(n_dr):
        blk = jnp.zeros((GRID_W, GRID_W), F32)
        for b in range(n_dc):
            blk = jnp.where(dc == b, rpb_ref[base + a * n_dc + b], blk)
        o_ref[a] = jnp.where(in_win, blk, NEG)


def _rpb_blocks(rpb):
    lh = rpb.shape[0] * rpb.shape[1]
    n_dr = 2 * NA_KH - 1
    return pl.pallas_call(
        _rpb_blocks_kernel,
        out_shape=jax.ShapeDtypeStruct((lh, n_dr, GRID_W, GRID_W), F32),
        grid=(lh,),
        in_specs=[pl.BlockSpec(memory_space=pltpu.SMEM)],
        out_specs=pl.BlockSpec((None, n_dr, GRID_W, GRID_W), lambda p: (p, 0, 0, 0)),
        compiler_params=_cparams(("arbitrary",)),
        name="rpb_blocks",
    )(rpb.reshape(-1))


def _nbr_window_plan(rows):
    nblk = rows // NBR_QROWS

    def plan(blk):
        r0 = blk * NBR_QROWS
        k0 = min(max(r0 - NA_KH // 2, 0), rows - NBR_KROWS)
        dr = np.zeros((NBR_QROWS, NBR_KROWS), np.int32)
        ok = np.zeros((NBR_QROWS, NBR_KROWS), bool)
        for qr in range(NBR_QROWS):
            r = r0 + qr
            rs = min(max(r - NA_KH // 2, 0), rows - NA_KH)
            for kr in range(NBR_KROWS):
                ok[qr, kr] = rs <= k0 + kr < rs + NA_KH
                dr[qr, kr] = min(max(k0 + kr - r + NA_KH - 1, 0), 2 * NA_KH - 2)
        return dr, ok

    classes = [plan(0), plan(1), plan(nblk - 1)]
    for blk in range(1, nblk - 1):
        dr, ok = plan(blk)
        assert (dr[ok] == classes[1][0][ok]).all() and (ok == classes[1][1]).all()
    return np.stack([c[0] for c in classes]), np.stack([c[1] for c in classes])


def _nbr_bias_tiles(blocks, rows):
    dr, ok = _nbr_window_plan(rows)
    t = blocks[:, dr]
    t = jnp.where(jnp.asarray(ok)[None, :, :, :, None, None], t, NEG)
    t = jnp.transpose(t, (0, 1, 2, 4, 3, 5))
    return t.reshape(blocks.shape[0], 3, BQ, NBR_KROWS * GRID_W)


def _nbr_kernel(q_ref, k_ref, v_ref, bias_ref, o_ref, *, n_lat, n_ctx, scale):
    i = pl.program_id(1)
    bq = q_ref.shape[0]
    rows = n_lat // GRID_W
    q = q_ref[...]
    k_ctx = k_ref[n_lat:n_lat + n_ctx, :]
    v_ctx = v_ref[n_lat:n_lat + n_ctx, :]
    s_ctx = _dot_nt(q, k_ctx) * scale
    m_ctx = jnp.max(s_ctx, axis=-1, keepdims=True)

    @pl.when(i * bq < n_lat)
    def _():
        k0 = jnp.clip(i * NBR_QROWS - NA_KH // 2, 0, rows - NBR_KROWS)
        start = pl.multiple_of(k0 * GRID_W, NBR_QROWS * GRID_W)
        nk = NBR_KROWS * GRID_W
        s_win = _dot_nt(q, k_ref[pl.ds(start, nk), :]) * scale + bias_ref[...]
        m = jnp.maximum(m_ctx, jnp.max(s_win, axis=-1, keepdims=True))
        p_win = jnp.exp(s_win - m)
        p_ctx = jnp.exp(s_ctx - m)
        l = jnp.sum(p_win, axis=-1, keepdims=True) + jnp.sum(p_ctx, axis=-1, keepdims=True)
        o = _dot(p_win.astype(BF16), v_ref[pl.ds(start, nk), :]) + _dot(p_ctx.astype(BF16), v_ctx)
        o_ref[...] = (o / l).astype(BF16)

    @pl.when(i * bq >= n_lat)
    def _():
        p_ctx = jnp.exp(s_ctx - m_ctx)
        l = jnp.sum(p_ctx, axis=-1, keepdims=True)
        o_ref[...] = (_dot(p_ctx.astype(BF16), v_ctx) / l).astype(BF16)


def _nbr_attention(qkv, bias, layer, *, n_lat, n_ctx, scale):
    ntok = qkv.shape[0]
    n_lat_blocks = n_lat // BQ
    nk = NBR_KROWS * GRID_W

    def bias_map(h, i):
        cls = jnp.where(i == 0, 0, jnp.where(i == n_lat_blocks - 1, 2, 1))
        return (layer * B_HEADS + h, cls, 0, 0)

    return pl.pallas_call(
        functools.partial(_nbr_kernel, n_lat=n_lat, n_ctx=n_ctx, scale=scale),
        out_shape=jax.ShapeDtypeStruct((ntok, B_HEADS * HEAD), BF16),
        grid=(B_HEADS, ntok // BQ),
        in_specs=[
            pl.BlockSpec((BQ, HEAD), lambda h, i: (i, QB0 + h)),
            pl.BlockSpec((ntok, HEAD), lambda h, i: (0, KB0 + h)),
            pl.BlockSpec((ntok, HEAD), lambda h, i: (0, VB0 + h)),
            pl.BlockSpec((None, None, BQ, nk), bias_map),
        ],
        out_specs=pl.BlockSpec((BQ, HEAD), lambda h, i: (i, h)),
        compiler_params=_cparams(("arbitrary", "arbitrary")),
        name="nbr_attn",
    )(qkv, qkv, qkv, bias)


def _merge_kernel(h_ref, mod_ref, g_ref, oa_ref, ob_ref, oc_ref, wg_ref, wa_ref, wb_ref, wc_ref,
                  o_ref, u_ref, *, n_lat):
    i, j = pl.program_id(0), pl.program_id(1)
    bm = h_ref.shape[0]

    @pl.when(j == 0)
    def _():
        u_ref[...] = _norm_modulate(h_ref[...], g_ref[...], mod_ref, 0, i * bm, n_lat)

    u = u_ref[...]
    merged = _sigmoid(_dot(u, wg_ref[0])) * _dot(oa_ref[...], wa_ref[...])
    merged += _sigmoid(_dot(u, wg_ref[1])) * _dot(ob_ref[...], wb_ref[...])
    merged += _sigmoid(_dot(u, wg_ref[2])) * _dot(oc_ref[...], wc_ref[...])
    o_ref[...] = merged.astype(BF16)


def _merge(h, mod, g, oa, ob, oc, wg, wa, wb, wc, n_lat):
    ntok, d = h.shape
    return pl.pallas_call(
        functools.partial(_merge_kernel, n_lat=n_lat),
        out_shape=jax.ShapeDtypeStruct((ntok, d), BF16),
        grid=(ntok // BM, d // BN_MERGE),
        in_specs=[
            pl.BlockSpec((BM, d), lambda i, j: (i, 0)),
            pl.BlockSpec(mod.shape, lambda i, j: (0, 0)),
            pl.BlockSpec((1, d), lambda i, j: (0, 0)),
            pl.BlockSpec((BM, oa.shape[1]), lambda i, j: (i, 0)),
            pl.BlockSpec((BM, ob.shape[1]), lambda i, j: (i, 0)),
            pl.BlockSpec((BM, oc.shape[1]), lambda i, j: (i, 0)),
            pl.BlockSpec((3, d, BN_MERGE), lambda i, j: (0, 0, j)),
            pl.BlockSpec((wa.shape[0], BN_MERGE), lambda i, j: (0, j)),
            pl.BlockSpec((wb.shape[0], BN_MERGE), lambda i, j: (0, j)),
            pl.BlockSpec((wc.shape[0], BN_MERGE), lambda i, j: (0, j)),
        ],
        out_specs=pl.BlockSpec((BM, BN_MERGE), lambda i, j: (i, j)),
        scratch_shapes=[pltpu.VMEM((BM, d), BF16)],
        compiler_params=_cparams(("arbitrary", "arbitrary")),
        name="branch_merge",
    )(h, mod, g, oa, ob, oc, wg, wa, wb, wc)


def _proj_res_kernel(a_ref, w_ref, h_ref, gate_ref, o_ref, *, n_lat):
    bm = h_ref.shape[0]
    rows = pl.program_id(0) * bm + lax.broadcasted_iota(jnp.int32, (bm, 1), 0)
    gate = jnp.where(rows >= n_lat, gate_ref[1:2, :], gate_ref[0:1, :])
    o_ref[...] = h_ref[...] + gate * _dot(a_ref[...], w_ref[...])


def _proj_res(a, w, h, mod, gate_col, n_lat, name):
    ntok, d = h.shape
    kdim = a.shape[1]
    return pl.pallas_call(
        functools.partial(_proj_res_kernel, n_lat=n_lat),
        out_shape=jax.ShapeDtypeStruct((ntok, d), F32),
        grid=(ntok // BM, d // BN),
        in_specs=[
            pl.BlockSpec((BM, kdim), lambda i, j: (i, 0)),
            pl.BlockSpec((kdim, BN), lambda i, j: (0, j)),
            pl.BlockSpec((BM, BN), lambda i, j: (i, j)),
            pl.BlockSpec((8, BN), lambda i, j: (0, gate_col // BN + j)),
        ],
        out_specs=pl.BlockSpec((BM, BN), lambda i, j: (i, j)),
        compiler_params=_cparams(("arbitrary", "arbitrary")),
        name=name,
    )(a, w, h, mod)


def _ffn_up_kernel(h_ref, hp_ref, hn_ref, mod_ref, g_ref, wa_ref, wg_ref, cwa_ref, cwg_ref,
                   cba_ref, cbg_ref, o_ref, u_ref, ya_ref, yg_ref, *, n_lat, n_tok):
    i, j = pl.program_id(0), pl.program_id(1)
    bm = h_ref.shape[0]
    d = h_ref.shape[1]
    col = 3 * d
    row0 = i * bm

    @pl.when(j == 0)
    def _():
        g = g_ref[...]
        u_ref[0:HALO, :] = _norm_modulate(hp_ref[...], g, mod_ref, col, row0 - HALO, n_lat)
        u_ref[HALO:HALO + bm, :] = _norm_modulate(h_ref[...], g, mod_ref, col, row0, n_lat)
        u_ref[HALO + bm:, :] = _norm_modulate(hn_ref[...], g, mod_ref, col, row0 + bm, n_lat)

    u = u_ref[...]
    ya_ref[...] = _dot(u, wa_ref[...])
    yg_ref[...] = _dot(u, wg_ref[...])

    rows = row0 + lax.broadcasted_iota(jnp.int32, (bm, 1), 0)
    has_prev = ((rows != 0) & (rows != n_lat)).astype(F32)
    has_next = ((rows != n_lat - 1) & (rows != n_tok - 1)).astype(F32)

    def conv(y_ref, cw_ref, cb_ref):
        prev = y_ref[pl.ds(HALO - 1, bm), :] * has_prev
        cur = y_ref[pl.ds(HALO, bm), :]
        nxt = y_ref[pl.ds(HALO + 1, bm), :] * has_next
        return cb_ref[...] + prev * cw_ref[0:1, :] + cur * cw_ref[1:2, :] + nxt * cw_ref[2:3, :]

    a = conv(ya_ref, cwa_ref, cba_ref)
    gt = conv(yg_ref, cwg_ref, cbg_ref)
    o_ref[...] = (gt * _sigmoid(gt) * a).astype(BF16)


def _ffn_up(h, mod, g, wa, wg, cwa, cwg, cba, cbg, n_lat):
    ntok, d = h.shape
    ff = wa.shape[1]
    per = BM // HALO
    last = ntok // HALO - 1
    return pl.pallas_call(
        functools.partial(_ffn_up_kernel, n_lat=n_lat, n_tok=ntok),
        out_shape=jax.ShapeDtypeStruct((ntok, ff), BF16),
        grid=(ntok // BM, ff // BN),
        in_specs=[
            pl.BlockSpec((BM, d), lambda i, j: (i, 0)),
            pl.BlockSpec((HALO, d), lambda i, j: (jnp.maximum(i * per - 1, 0), 0)),
            pl.BlockSpec((HALO, d), lambda i, j: (jnp.minimum((i + 1) * per, last), 0)),
            pl.BlockSpec(mod.shape, lambda i, j: (0, 0)),
            pl.BlockSpec((1, d), lambda i, j: (0, 0)),
            pl.BlockSpec((d, BN), lambda i, j: (0, j)),
            pl.BlockSpec((d, BN), lambda i, j: (0, j)),
            pl.BlockSpec((3, BN), lambda i, j: (0, j)),
            pl.BlockSpec((3, BN), lambda i, j: (0, j)),
            pl.BlockSpec((1, BN), lambda i, j: (0, j)),
            pl.BlockSpec((1, BN), lambda i, j: (0, j)),
        ],
        out_specs=pl.BlockSpec((BM, BN), lambda i, j: (i, j)),
        scratch_shapes=[
            pltpu.VMEM((BM + 2 * HALO, d), BF16),
            pltpu.VMEM((BM + 2 * HALO, BN), F32),
            pltpu.VMEM((BM + 2 * HALO, BN), F32),
        ],
        compiler_params=_cparams(("arbitrary", "arbitrary")),
        name="ffn_up",
    )(h, h, h, mod, g, wa, wg, cwa, cwg, cba, cbg)


def _rope_tables(n_lat, n_ctx, dim):
    n_freq = dim // 4
    inv_freq = ROPE_THETA ** (-jnp.arange(n_freq, dtype=F32) / n_freq)
    t = jnp.arange(n_lat, dtype=jnp.int32)
    pos = jnp.stack([t // GRID_W, t % GRID_W], axis=-1).astype(F32)
    ang = pos[:, :, None] * inv_freq
    cos, sin = jnp.cos(ang), jnp.sin(ang)
    cos_l = jnp.concatenate([cos[:, 0], cos[:, 0], cos[:, 1], cos[:, 1]], axis=-1)
    sin_l = jnp.concatenate([-sin[:, 0], sin[:, 0], -sin[:, 1], sin[:, 1]], axis=-1)
    rep = HEAD // dim
    cos_l = jnp.tile(cos_l, (1, rep))
    sin_l = jnp.tile(sin_l, (1, rep))
    cos_l = jnp.concatenate([cos_l, jnp.ones((n_ctx, HEAD), F32)], axis=0)
    sin_l = jnp.concatenate([sin_l, jnp.zeros((n_ctx, HEAD), F32)], axis=0)
    return cos_l, sin_l


def kernel(x, c, ctx, c_ctx, w_mod, b_mod, g_norm1, w_in, gq_a, gk_a, gq_b, gk_b, rpb_b, gq_c, gk_c, lam_q1, lam_k1, lam_q2, lam_k2, g_subln_c, w_br_a, w_br_b, w_br_c, w_o, g_norm2, w_up, conv_w, conv_b, w_down):
    batch, n_lat, d = x.shape
    n_ctx = ctx.shape[1]
    depth = w_mod.shape[0]
    d_ff = w_down.shape[1]
    assert batch == 1 and d == 16 * HEAD
    assert (n_lat + n_ctx) % BM == 0 and n_lat % BK == 0 and n_lat % BQ == 0 and n_ctx <= BQ
    ff_pad = -(-d_ff // BN) * BN

    wqkv = w_in[:, :, :N_QKV].astype(BF16)
    wgate = jnp.transpose(w_in[:, :, N_QKV:].reshape(depth, d, 3, d), (0, 2, 1, 3)).astype(BF16)
    wbra, wbrb, wbrc, wo = (w.astype(BF16) for w in (w_br_a, w_br_b, w_br_c, w_o))
    padc = ((0, 0), (0, 0), (0, ff_pad - d_ff))
    wupa = jnp.pad(w_up[:, :, :d_ff].astype(BF16), padc)
    wupg = jnp.pad(w_up[:, :, d_ff:].astype(BF16), padc)
    cwa = jnp.pad(conv_w[:, :, :d_ff], padc)
    cwg = jnp.pad(conv_w[:, :, d_ff:], padc)
    cba = jnp.pad(conv_b[:, None, :d_ff], padc)
    cbg = jnp.pad(conv_b[:, None, d_ff:], padc)
    wdown = jnp.pad(w_down.astype(BF16), ((0, 0), (0, ff_pad - d_ff), (0, 0)))

    ones = jnp.ones((depth, HEAD), F32)
    gains = jnp.stack([gq_a, gk_a, gq_b, gk_b, jnp.tile(gq_c, (1, 2)), jnp.tile(gk_c, (1, 2)), ones, ones], axis=1)
    pad64 = lambda v: jnp.pad(v, ((0, 0), (0, HEAD - C_HALF)))
    lam_init = jnp.asarray([0.8 - 0.6 * math.exp(-0.3 * l) for l in range(depth)], F32)
    cpar = jnp.stack([pad64(lam_q1), pad64(lam_k1), pad64(lam_q2), pad64(lam_k2), g_subln_c,
                      jnp.broadcast_to(lam_init[:, None], (depth, HEAD)), ones, ones], axis=1)
    apar = jnp.zeros((8, HEAD), F32)

    rope = _rope_tables(n_lat, n_ctx, HEAD) + _rope_tables(n_lat, n_ctx, C_HALF)
    cc = jnp.zeros((8, d), F32).at[0].set(c[0]).at[1].set(c_ctx)

    mod = _modulation(cc, w_mod, b_mod)
    bias = _nbr_bias_tiles(_rpb_blocks(rpb_b), n_lat // GRID_W)

    h = jnp.concatenate([x[0], ctx[0]], axis=0)
    att = dict(n_lat=n_lat, n_ctx=n_ctx)
    for l in range(depth):
        g1 = g_norm1[l][None]
        qkv = _qkv_proj(h, mod[l], g1, wqkv[l], gains[l], rope, n_lat)
        oa = _flash(qkv, apar, n_kv_heads=A_KV, groups=A_HEADS // A_KV, diff=False,
                    q0=QA0, k0=KA0, v0=VA0, scale=HEAD ** -0.5, **att)
        ob = _nbr_attention(qkv, bias, l, scale=HEAD ** -0.5, **att)
        oc = _flash(qkv, cpar[l], n_kv_heads=C_HEADS, groups=2, diff=True,
                    q0=QC0, k0=KC0, v0=VC0, scale=C_HALF ** -0.5, **att)
        merged = _merge(h, mod[l], g1, oa, ob, oc, wgate[l], wbra[l], wbrb[l], wbrc[l], n_lat)
        h = _proj_res(merged, wo[l], h, mod[l], 2 * d, n_lat, "out_proj")
        act = _ffn_up(h, mod[l], g_norm2[l][None], wupa[l], wupg[l], cwa[l], cwg[l], cba[l], cbg[l], n_lat)
        h = _proj_res(act, wdown[l], h, mod[l], 5 * d, n_lat, "ffn_down")
    return h[:n_lat][None]
```

```python
import functools
import math

import numpy as np
import jax
import jax.numpy as jnp
from jax import lax
from jax.experimental import pallas as pl
from jax.experimental.pallas import tpu as pltpu

F32 = jnp.float32
BF16 = jnp.bfloat16

HEAD = 128
GRID_W = 64
A_HEADS, A_KV = 8, 2
B_HEADS = 4
C_HEADS = 4
C_HALF = 64
NA_KH, NA_KW = 8, 16
ROPE_THETA = 10000.0
EPS = 1e-6
NEG = -1e30
LOG2E = math.log2(math.e)
Q_SCALE_AB = HEAD ** -0.5 * LOG2E
Q_SCALE_C = C_HALF ** -0.5 * LOG2E

QA0, KA0, VA0 = 0, 8, 10
QB0, KB0, VB0 = 12, 16, 20
QC0, KC0, VC0 = 24, 28, 32
N_QKV = 36 * HEAD

VMEM_LIMIT = 56 * 1024 * 1024
MXU_DIM = 256
BM = 768
BN = 512
BN_MERGE = 256
HALO = 16
BQ = 256
BK_GQA = 2048
BK_DIFF = 2048
NBR_QROWS = BQ // GRID_W
NBR_KROWS = NBR_QROWS + NA_KH
FAST_BOUND = 48.0


def _cparams(sem):
    return pltpu.CompilerParams(dimension_semantics=sem, vmem_limit_bytes=VMEM_LIMIT)


def _sigmoid(x):
    return 1.0 / (1.0 + jnp.exp(-x))


def _dot(a, b):
    return jnp.dot(a, b, preferred_element_type=F32)


def _dot_nt(a, b):
    return lax.dot_general(a, b, (((1,), (1,)), ((), ())), preferred_element_type=F32)


def _mod_kernel(cc_ref, w_ref, b_ref, o_ref):
    cc = cc_ref[...]
    a = (cc * _sigmoid(cc)).astype(BF16)
    o_ref[...] = _dot(a, w_ref[...].astype(BF16)) + b_ref[...]


def _modulation(cc, w_mod, b_mod):
    depth, d, n = w_mod.shape
    bn = 1024
    return pl.pallas_call(
        _mod_kernel,
        out_shape=jax.ShapeDtypeStruct((depth, 8, n), F32),
        grid=(depth, n // bn),
        in_specs=[
            pl.BlockSpec((8, d), lambda l, j: (0, 0)),
            pl.BlockSpec((None, d, bn), lambda l, j: (l, 0, j)),
            pl.BlockSpec((None, 1, bn), lambda l, j: (l, 0, j)),
        ],
        out_specs=pl.BlockSpec((None, 8, bn), lambda l, j: (l, 0, j)),
        compiler_params=_cparams(("arbitrary", "arbitrary")),
        name="adaln_mod",
    )(cc, w_mod, b_mod.reshape(depth, 1, n))


def _norm_modulate(x, g, mod_ref, col0, row0, n_lat):
    d = x.shape[-1]
    y = x * lax.rsqrt(jnp.mean(x * x, axis=-1, keepdims=True) + EPS) * g
    rows = row0 + lax.broadcasted_iota(jnp.int32, (x.shape[0], 1), 0)
    is_ctx = rows >= n_lat
    shift = jnp.where(is_ctx, mod_ref[1:2, col0:col0 + d], mod_ref[0:1, col0:col0 + d])
    scale = jnp.where(is_ctx, mod_ref[1:2, col0 + d:col0 + 2 * d], mod_ref[0:1, col0 + d:col0 + 2 * d])
    return (y * (1.0 + scale) + shift).astype(BF16)


def _rms_heads(x, gain, group):
    x2 = x * x
    if group == HEAD:
        ms = jnp.mean(x2, axis=-1, keepdims=True)
    else:
        lane = lax.broadcasted_iota(jnp.int32, x.shape, 1)
        lo = lane < group
        s_lo = jnp.sum(jnp.where(lo, x2, 0.0), axis=-1, keepdims=True)
        s_hi = jnp.sum(jnp.where(lo, 0.0, x2), axis=-1, keepdims=True)
        ms = jnp.where(lo, s_lo, s_hi) * (1.0 / group)
    return x * lax.rsqrt(ms + EPS) * gain


def _rope(x, cos, sin_signed, half):
    lane = lax.broadcasted_iota(jnp.int32, x.shape, 1)
    first = (lane % (2 * half)) < half
    partner = jnp.where(first, pltpu.roll(x, HEAD - half, 1), pltpu.roll(x, half, 1))
    return x * cos + partner * sin_signed


_RAW = ("raw", 0, 1.0)
_QKV_BLOCKS = (
    (("rope_a", 0, Q_SCALE_AB),) * 4, (("rope_a", 0, Q_SCALE_AB),) * 4,
    (("rope_a", 1, 1.0), ("rope_a", 1, 1.0), _RAW, _RAW),
    (("norm", 2, Q_SCALE_AB),) * 4, (("norm", 3, 1.0),) * 4, (_RAW,) * 4,
    (("rope_c", 4, Q_SCALE_C),) * 4, (("rope_c", 5, 1.0),) * 4, (_RAW,) * 4,
)


def _qkv_kernel(h_ref, mod_ref, g_ref, w_ref, gains_ref, ca_ref, sa_ref, cc_ref, sc_ref,
                o_ref, u_ref, *, n_lat):
    i, j = pl.program_id(0), pl.program_id(1)
    bm = h_ref.shape[0]

    @pl.when(j == 0)
    def _():
        u_ref[...] = _norm_modulate(h_ref[...], g_ref[...], mod_ref, 0, i * bm, n_lat)

    acc = _dot(u_ref[...], w_ref[...])

    def epilogue(heads):
        for k, (kind, gr, out_scale) in enumerate(heads):
            x = acc[:, k * HEAD:(k + 1) * HEAD]
            gain = gains_ref[gr:gr + 1, :]
            if kind == "rope_a":
                x = _rope(_rms_heads(x, gain, HEAD), ca_ref[...], sa_ref[...], HEAD // 4)
            elif kind == "norm":
                x = _rms_heads(x, gain, HEAD)
            elif kind == "rope_c":
                x = _rope(_rms_heads(x, gain, C_HALF), cc_ref[...], sc_ref[...], C_HALF // 4)
            if out_scale != 1.0:
                x = x * out_scale
            o_ref[:, k * HEAD:(k + 1) * HEAD] = x.astype(BF16)

    for heads in sorted(set(_QKV_BLOCKS)):
        cond = None
        for jj, hh in enumerate(_QKV_BLOCKS):
            if hh == heads:
                cond = (j == jj) if cond is None else (cond | (j == jj))
        pl.when(cond)(functools.partial(epilogue, heads))


def _qkv_proj(h, mod, g, w, gains, rope, n_lat):
    ntok, d = h.shape
    n = w.shape[1]
    ca, sa, cc, sc = rope
    return pl.pallas_call(
        functools.partial(_qkv_kernel, n_lat=n_lat),
        out_shape=jax.ShapeDtypeStruct((ntok, n), BF16),
        grid=(ntok // BM, n // BN),
        in_specs=[
            pl.BlockSpec((BM, d), lambda i, j: (i, 0)),
            pl.BlockSpec(mod.shape, lambda i, j: (0, 0)),
            pl.BlockSpec((1, d), lambda i, j: (0, 0)),
            pl.BlockSpec((d, BN), lambda i, j: (0, j)),
            pl.BlockSpec(gains.shape, lambda i, j: (0, 0)),
            pl.BlockSpec((BM, HEAD), lambda i, j: (i, 0)),
            pl.BlockSpec((BM, HEAD), lambda i, j: (i, 0)),
            pl.BlockSpec((BM, HEAD), lambda i, j: (i, 0)),
            pl.BlockSpec((BM, HEAD), lambda i, j: (i, 0)),
        ],
        out_specs=pl.BlockSpec((BM, BN), lambda i, j: (i, j)),
        scratch_shapes=[pltpu.VMEM((BM, d), BF16)],
        compiler_params=_cparams(("arbitrary", "arbitrary")),
        name="qkv_proj",
    )(h, mod, g, w, gains, ca, sa, cc, sc)


def _flash_kernel(q_ref, k_ref, v_ref, par_ref, o_ref,
                  ka_ref, va_ref, kmax_ref, qa_ref, acc_ref, m_ref, l_ref, acc2_ref,
                  *, heads, groups, diff, bk, n_lat, n_ctx):
    i = pl.program_id(1)
    bq = q_ref.shape[0]
    ntok = k_ref.shape[0]
    qw = q_ref.shape[1] // heads
    is_lat = i * bq < n_lat
    n_main = jnp.where(is_lat, n_lat // bk - 1, 0)
    last0 = n_lat - bk

    def lane_sets(shape):
        if not diff:
            return [None] * groups
        lane = lax.broadcasted_iota(jnp.int32, shape, 1)
        return [lane < C_HALF, lane >= C_HALF]

    @pl.when(i == 0)
    def _():
        unit = (lax.broadcasted_iota(jnp.int32, (ntok, HEAD), 1) == 0).astype(F32).astype(BF16)
        for h in range(heads):
            k = k_ref[:, h * HEAD:(h + 1) * HEAD]
            ka_ref[h, :, :HEAD] = k
            ka_ref[h, :, HEAD:] = unit
            va_ref[h, :, :HEAD] = v_ref[:, h * HEAD:(h + 1) * HEAD]
            va_ref[h, :, HEAD:] = unit
            k2 = k.astype(F32)
            k2 = k2 * k2
            for g, lanes in enumerate(lane_sets(k2.shape)[:2 if diff else 1]):
                kk = k2 if lanes is None else jnp.where(lanes, k2, 0.0)
                kmax_ref[2 * h + g] = jnp.max(jnp.sum(kk, axis=-1, keepdims=True), axis=0, keepdims=True)

    lane0 = lax.broadcasted_iota(jnp.int32, (bq, HEAD), 1) == 0
    bound = None
    for h in range(heads):
        for g, lanes in enumerate(lane_sets((bq, HEAD))):
            if diff:
                qv = q_ref[:, h * qw:(h + 1) * qw]
                qg = jnp.where(lanes, qv, jnp.zeros_like(qv))
            else:
                qg = q_ref[:, h * qw + g * HEAD:h * qw + (g + 1) * HEAD]
            qf = qg.astype(F32)
            b = jnp.sqrt(jnp.sum(qf * qf, axis=-1, keepdims=True) * kmax_ref[2 * h + (g if diff else 0)])
            qa_ref[h, g * bq:(g + 1) * bq, :HEAD] = qg
            qa_ref[h, g * bq:(g + 1) * bq, HEAD:] = jnp.where(lane0, -b, 0.0).astype(BF16)
            bound = b if bound is None else jnp.maximum(bound, b)
    fast = jnp.max(bound) <= FAST_BOUND

    def finalize(get):
        for h in range(heads):
            if diff:
                lam = (jnp.exp(jnp.sum(par_ref[0:1, :] * par_ref[1:2, :], axis=-1, keepdims=True))
                       - jnp.exp(jnp.sum(par_ref[2:3, :] * par_ref[3:4, :], axis=-1, keepdims=True))
                       + par_ref[5:6, 0:1])
                (a0, l0), (a1, l1) = get(h, 0), get(h, 1)
                o = a0 / l0 - lam * (a1 / l1)
                o = o * lax.rsqrt(jnp.mean(o * o, axis=-1, keepdims=True) + EPS) * par_ref[4:5, :]
                o_ref[:, h * qw:(h + 1) * qw] = (o * (1.0 - par_ref[5:6, 0:1])).astype(BF16)
            else:
                for g in range(groups):
                    a, l = get(h, g)
                    o_ref[:, h * qw + g * HEAD:h * qw + (g + 1) * HEAD] = (a / l).astype(BF16)

    def sweep(chunk, kr, vr):
        def body(c, carry):
            sl = pl.ds(pl.multiple_of(c * bk, bk), bk)
            chunk(lambda h: kr(h, sl), lambda h: vr(h, sl))
            return carry

        lax.fori_loop(0, n_main, body, 0)

        @pl.when(is_lat)
        def _():
            sl = slice(last0, n_lat + n_ctx)
            chunk(lambda h: kr(h, sl), lambda h: vr(h, sl))

        @pl.when(jnp.logical_not(is_lat))
        def _():
            sl = slice(n_lat, n_lat + n_ctx)
            chunk(lambda h: kr(h, sl), lambda h: vr(h, sl))

    @pl.when(fast)
    def _():
        acc_ref[...] = jnp.zeros(acc_ref.shape, F32)

        def chunk(k, v):
            for h in range(heads):
                p = jnp.exp2(_dot_nt(qa_ref[h], k(h))).astype(BF16)
                acc_ref[h] += _dot(p, v(h))

        sweep(chunk, lambda h, sl: ka_ref[h, sl, :], lambda h, sl: va_ref[h, sl, :])
        finalize(lambda h, g: (acc_ref[h, g * bq:(g + 1) * bq, :HEAD],
                               acc_ref[h, g * bq:(g + 1) * bq, HEAD:HEAD + 1]))

    @pl.when(jnp.logical_not(fast))
    def _():
        m_ref[...] = jnp.full(m_ref.shape, NEG, F32)
        l_ref[...] = jnp.zeros(l_ref.shape, F32)
        acc2_ref[...] = jnp.zeros(acc2_ref.shape, F32)

        def chunk(k, v):
            for h in range(heads):
                for g in range(groups):
                    r = h * groups + g
                    s = _dot_nt(qa_ref[h, g * bq:(g + 1) * bq, :HEAD], k(h))
                    m_prev = m_ref[r]
                    m_new = jnp.maximum(m_prev, jnp.max(s, axis=-1, keepdims=True))
                    alpha = jnp.exp2(m_prev - m_new)
                    p = jnp.exp2(s - m_new)
                    l_ref[r] = alpha * l_ref[r] + jnp.sum(p, axis=-1, keepdims=True)
                    acc2_ref[r] = alpha * acc2_ref[r] + _dot(p.astype(BF16), v(h))
                    m_ref[r] = m_new

        sweep(chunk, lambda h, sl: k_ref[sl, h * HEAD:(h + 1) * HEAD],
              lambda h, sl: v_ref[sl, h * HEAD:(h + 1) * HEAD])
        finalize(lambda h, g: (acc2_ref[h * groups + g], l_ref[h * groups + g]))


def _flash(qkv, par, *, n_kv_heads, heads, groups, diff, bk, q0, k0, v0, n_lat, n_ctx):
    ntok = qkv.shape[0]
    qw = heads * (HEAD if diff else groups * HEAD)
    kw = heads * HEAD
    return pl.pallas_call(
        functools.partial(_flash_kernel, heads=heads, groups=groups, diff=diff, bk=bk,
                          n_lat=n_lat, n_ctx=n_ctx),
        out_shape=jax.ShapeDtypeStruct((ntok, n_kv_heads // heads * qw), BF16),
        grid=(n_kv_heads // heads, ntok // BQ),
        in_specs=[
            pl.BlockSpec((BQ, qw), lambda h, i: (i, q0 * HEAD // qw + h)),
            pl.BlockSpec((ntok, kw), lambda h, i: (0, k0 * HEAD // kw + h)),
            pl.BlockSpec((ntok, kw), lambda h, i: (0, v0 * HEAD // kw + h)),
            pl.BlockSpec(par.shape, lambda h, i: (0, 0)),
        ],
        out_specs=pl.BlockSpec((BQ, qw), lambda h, i: (i, h)),
        scratch_shapes=[
            pltpu.VMEM((heads, ntok, MXU_DIM), BF16),
            pltpu.VMEM((heads, ntok, MXU_DIM), BF16),
            pltpu.VMEM((2 * heads, 1, 1), F32),
            pltpu.VMEM((heads, groups * BQ, MXU_DIM), BF16),
            pltpu.VMEM((heads, groups * BQ, MXU_DIM), F32),
            pltpu.VMEM((heads * groups, BQ, 1), F32),
            pltpu.VMEM((heads * groups, BQ, 1), F32),
            pltpu.VMEM((heads * groups, BQ, HEAD), F32),
        ],
        compiler_params=_cparams(("arbitrary", "arbitrary")),
        name="flash_diff" if diff else "flash_gqa",
    )(qkv, qkv, qkv, par)


def _rpb_blocks_kernel(rpb_ref, o_ref):
    n_dr, n_dc = 2 * NA_KH - 1, 2 * NA_KW - 1
    base = pl.program_id(0) * (n_dr * n_dc)
    qc = lax.broadcasted_iota(jnp.int32, (GRID_W, GRID_W), 0)
    kc = lax.broadcasted_iota(jnp.int32, (GRID_W, GRID_W), 1)
    cs = jnp.clip(qc - NA_KW // 2, 0, GRID_W - NA_KW)
    in_win = (kc >= cs) & (kc < cs + NA_KW)
    dc = kc - qc + (NA_KW - 1)
    for a in range(n_dr):
        blk = jnp.zeros((GRID_W, GRID_W), F32)
        for b in range(n_dc):
            blk = jnp.where(dc == b, rpb_ref[base + a * n_dc + b], blk)
        o_ref[a] = jnp.where(in_win, blk * LOG2E, NEG)


def _rpb_blocks(rpb):
    lh = rpb.shape[0] * rpb.shape[1]
    n_dr = 2 * NA_KH - 1
    return pl.pallas_call(
        _rpb_blocks_kernel,
        out_shape=jax.ShapeDtypeStruct((lh, n_dr, GRID_W, GRID_W), F32),
        grid=(lh,),
        in_specs=[pl.BlockSpec(memory_space=pltpu.SMEM)],
        out_specs=pl.BlockSpec((None, n_dr, GRID_W, GRID_W), lambda p: (p, 0, 0, 0)),
        compiler_params=_cparams(("arbitrary",)),
        name="rpb_blocks",
    )(rpb.reshape(-1))


def _nbr_window_plan(rows):
    nblk = rows // NBR_QROWS

    def plan(blk):
        r0 = blk * NBR_QROWS
        k0 = min(max(r0 - NA_KH // 2, 0), rows - NBR_KROWS)
        dr = np.zeros((NBR_QROWS, NBR_KROWS), np.int32)
        ok = np.zeros((NBR_QROWS, NBR_KROWS), bool)
        for qr in range(NBR_QROWS):
            r = r0 + qr
            rs = min(max(r - NA_KH // 2, 0), rows - NA_KH)
            for kr in range(NBR_KROWS):
                ok[qr, kr] = rs <= k0 + kr < rs + NA_KH
                dr[qr, kr] = min(max(k0 + kr - r + NA_KH - 1, 0), 2 * NA_KH - 2)
        return dr, ok

    classes = [plan(0), plan(1), plan(nblk - 1)]
    for blk in range(1, nblk - 1):
        dr, ok = plan(blk)
        assert (dr[ok] == classes[1][0][ok]).all() and (ok == classes[1][1]).all()
    return np.stack([c[0] for c in classes]), np.stack([c[1] for c in classes])


def _nbr_bias_tiles(blocks, rows):
    dr, ok = _nbr_window_plan(rows)
    t = blocks[:, dr]
    t = jnp.where(jnp.asarray(ok)[None, :, :, :, None, None], t, NEG)
    t = jnp.transpose(t, (0, 1, 2, 4, 3, 5))
    return t.reshape(blocks.shape[0], 3, BQ, NBR_KROWS * GRID_W)


def _nbr_kernel(q_ref, k_ref, v_ref, bias_ref, o_ref, *, n_lat, n_ctx):
    i = pl.program_id(1)
    bq = q_ref.shape[0]
    rows = n_lat // GRID_W
    q = q_ref[...]
    k_ctx = k_ref[n_lat:n_lat + n_ctx, :]
    v_ctx = v_ref[n_lat:n_lat + n_ctx, :]
    s_ctx = _dot_nt(q, k_ctx)
    m_ctx = jnp.max(s_ctx, axis=-1, keepdims=True)

    @pl.when(i * bq < n_lat)
    def _():
        k0 = jnp.clip(i * NBR_QROWS - NA_KH // 2, 0, rows - NBR_KROWS)
        start = pl.multiple_of(k0 * GRID_W, NBR_QROWS * GRID_W)
        nk = NBR_KROWS * GRID_W
        s_win = _dot_nt(q, k_ref[pl.ds(start, nk), :]) + bias_ref[...]
        m = jnp.maximum(m_ctx, jnp.max(s_win, axis=-1, keepdims=True))
        p_win = jnp.exp2(s_win - m)
        p_ctx = jnp.exp2(s_ctx - m)
        l = jnp.sum(p_win, axis=-1, keepdims=True) + jnp.sum(p_ctx, axis=-1, keepdims=True)
        o = _dot(p_win.astype(BF16), v_ref[pl.ds(start, nk), :]) + _dot(p_ctx.astype(BF16), v_ctx)
        o_ref[...] = (o / l).astype(BF16)

    @pl.when(i * bq >= n_lat)
    def _():
        p_ctx = jnp.exp2(s_ctx - m_ctx)
        l = jnp.sum(p_ctx, axis=-1, keepdims=True)
        o_ref[...] = (_dot(p_ctx.astype(BF16), v_ctx) / l).astype(BF16)


def _nbr_attention(qkv, bias, layer, *, n_lat, n_ctx):
    ntok = qkv.shape[0]
    n_lat_blocks = n_lat // BQ
    nk = NBR_KROWS * GRID_W

    def bias_map(h, i):
        cls = jnp.where(i == 0, 0, jnp.where(i == n_lat_blocks - 1, 2, 1))
        return (layer * B_HEADS + h, cls, 0, 0)

    return pl.pallas_call(
        functools.partial(_nbr_kernel, n_lat=n_lat, n_ctx=n_ctx),
        out_shape=jax.ShapeDtypeStruct((ntok, B_HEADS * HEAD), BF16),
        grid=(B_HEADS, ntok // BQ),
        in_specs=[
            pl.BlockSpec((BQ, HEAD), lambda h, i: (i, QB0 + h)),
            pl.BlockSpec((ntok, HEAD), lambda h, i: (0, KB0 + h)),
            pl.BlockSpec((ntok, HEAD), lambda h, i: (0, VB0 + h)),
            pl.BlockSpec((None, None, BQ, nk), bias_map),
        ],
        out_specs=pl.BlockSpec((BQ, HEAD), lambda h, i: (i, h)),
        compiler_params=_cparams(("arbitrary", "arbitrary")),
        name="nbr_attn",
    )(qkv, qkv, qkv, bias)


def _merge_kernel(h_ref, mod_ref, g_ref, oa_ref, ob_ref, oc_ref, wg_ref, wa_ref, wb_ref, wc_ref,
                  o_ref, u_ref, *, n_lat):
    i, j = pl.program_id(0), pl.program_id(1)
    bm = h_ref.shape[0]

    @pl.when(j == 0)
    def _():
        u_ref[...] = _norm_modulate(h_ref[...], g_ref[...], mod_ref, 0, i * bm, n_lat)

    u = u_ref[...]
    merged = _sigmoid(_dot(u, wg_ref[0])) * _dot(oa_ref[...], wa_ref[...])
    merged += _sigmoid(_dot(u, wg_ref[1])) * _dot(ob_ref[...], wb_ref[...])
    merged += _sigmoid(_dot(u, wg_ref[2])) * _dot(oc_ref[...], wc_ref[...])
    o_ref[...] = merged.astype(BF16)


def _merge(h, mod, g, oa, ob, oc, wg, wa, wb, wc, n_lat):
    ntok, d = h.shape
    return pl.pallas_call(
        functools.partial(_merge_kernel, n_lat=n_lat),
        out_shape=jax.ShapeDtypeStruct((ntok, d), BF16),
        grid=(ntok // BM, d // BN_MERGE),
        in_specs=[
            pl.BlockSpec((BM, d), lambda i, j: (i, 0)),
            pl.BlockSpec(mod.shape, lambda i, j: (0, 0)),
            pl.BlockSpec((1, d), lambda i, j: (0, 0)),
            pl.BlockSpec((BM, oa.shape[1]), lambda i, j: (i, 0)),
            pl.BlockSpec((BM, ob.shape[1]), lambda i, j: (i, 0)),
            pl.BlockSpec((BM, oc.shape[1]), lambda i, j: (i, 0)),
            pl.BlockSpec((3, d, BN_MERGE), lambda i, j: (0, 0, j)),
            pl.BlockSpec((wa.shape[0], BN_MERGE), lambda i, j: (0, j)),
            pl.BlockSpec((wb.shape[0], BN_MERGE), lambda i, j: (0, j)),
            pl.BlockSpec((wc.shape[0], BN_MERGE), lambda i, j: (0, j)),
        ],
        out_specs=pl.BlockSpec((BM, BN_MERGE), lambda i, j: (i, j)),
        scratch_shapes=[pltpu.VMEM((BM, d), BF16)],
        compiler_params=_cparams(("arbitrary", "arbitrary")),
        name="branch_merge",
    )(h, mod, g, oa, ob, oc, wg, wa, wb, wc)


def _proj_res_kernel(a_ref, w_ref, h_ref, gate_ref, o_ref, *, n_lat):
    bm = h_ref.shape[0]
    rows = pl.program_id(0) * bm + lax.broadcasted_iota(jnp.int32, (bm, 1), 0)
    gate = jnp.where(rows >= n_lat, gate_ref[1:2, :], gate_ref[0:1, :])
    o_ref[...] = h_ref[...] + gate * _dot(a_ref[...], w_ref[...])


def _proj_res(a, w, h, mod, gate_col, n_lat, name):
    ntok, d = h.shape
    kdim = a.shape[1]
    return pl.pallas_call(
        functools.partial(_proj_res_kernel, n_lat=n_lat),
        out_shape=jax.ShapeDtypeStruct((ntok, d), F32),
        grid=(ntok // BM, d // BN),
        in_specs=[
            pl.BlockSpec((BM, kdim), lambda i, j: (i, 0)),
            pl.BlockSpec((kdim, BN), lambda i, j: (0, j)),
            pl.BlockSpec((BM, BN), lambda i, j: (i, j)),
            pl.BlockSpec((8, BN), lambda i, j: (0, gate_col // BN + j)),
        ],
        out_specs=pl.BlockSpec((BM, BN), lambda i, j: (i, j)),
        compiler_params=_cparams(("arbitrary", "arbitrary")),
        name=name,
    )(a, w, h, mod)


def _ffn_up_kernel(h_ref, hp_ref, hn_ref, mod_ref, g_ref, wa_ref, wg_ref, cwa_ref, cwg_ref,
                   cba_ref, cbg_ref, o_ref, u_ref, ya_ref, yg_ref, *, n_lat, n_tok):
    i, j = pl.program_id(0), pl.program_id(1)
    bm = h_ref.shape[0]
    d = h_ref.shape[1]
    col = 3 * d
    row0 = i * bm

    @pl.when(j == 0)
    def _():
        g = g_ref[...]
        u_ref[0:HALO, :] = _norm_modulate(hp_ref[...], g, mod_ref, col, row0 - HALO, n_lat)
        u_ref[HALO:HALO + bm, :] = _norm_modulate(h_ref[...], g, mod_ref, col, row0, n_lat)
        u_ref[HALO + bm:, :] = _norm_modulate(hn_ref[...], g, mod_ref, col, row0 + bm, n_lat)

    u = u_ref[...]
    ya_ref[...] = _dot(u, wa_ref[...])
    yg_ref[...] = _dot(u, wg_ref[...])

    rows = row0 + lax.broadcasted_iota(jnp.int32, (bm, 1), 0)
    has_prev = ((rows != 0) & (rows != n_lat)).astype(F32)
    has_next = ((rows != n_lat - 1) & (rows != n_tok - 1)).astype(F32)

    def conv(y_ref, cw_ref, cb_ref):
        prev = y_ref[pl.ds(HALO - 1, bm), :] * has_prev
        cur = y_ref[pl.ds(HALO, bm), :]
        nxt = y_ref[pl.ds(HALO + 1, bm), :] * has_next
        return cb_ref[...] + prev * cw_ref[0:1, :] + cur * cw_ref[1:2, :] + nxt * cw_ref[2:3, :]

    a = conv(ya_ref, cwa_ref, cba_ref)
    gt = conv(yg_ref, cwg_ref, cbg_ref)
    o_ref[...] = (gt * _sigmoid(gt) * a).astype(BF16)


def _ffn_up(h, mod, g, wa, wg, cwa, cwg, cba, cbg, n_lat):
    ntok, d = h.shape
    ff = wa.shape[1]
    per = BM // HALO
    last = ntok // HALO - 1
    return pl.pallas_call(
        functools.partial(_ffn_up_kernel, n_lat=n_lat, n_tok=ntok),
        out_shape=jax.ShapeDtypeStruct((ntok, ff), BF16),
        grid=(ntok // BM, ff // BN),
        in_specs=[
            pl.BlockSpec((BM, d), lambda i, j: (i, 0)),
            pl.BlockSpec((HALO, d), lambda i, j: (jnp.maximum(i * per - 1, 0), 0)),
            pl.BlockSpec((HALO, d), lambda i, j: (jnp.minimum((i + 1) * per, last), 0)),
            pl.BlockSpec(mod.shape, lambda i, j: (0, 0)),
            pl.BlockSpec((1, d), lambda i, j: (0, 0)),
            pl.BlockSpec((d, BN), lambda i, j: (0, j)),
            pl.BlockSpec((d, BN), lambda i, j: (0, j)),
            pl.BlockSpec((3, BN), lambda i, j: (0, j)),
            pl.BlockSpec((3, BN), lambda i, j: (0, j)),
            pl.BlockSpec((1, BN), lambda i, j: (0, j)),
            pl.BlockSpec((1, BN), lambda i, j: (0, j)),
        ],
        out_specs=pl.BlockSpec((BM, BN), lambda i, j: (i, j)),
        scratch_shapes=[
            pltpu.VMEM((BM + 2 * HALO, d), BF16),
            pltpu.VMEM((BM + 2 * HALO, BN), F32),
            pltpu.VMEM((BM + 2 * HALO, BN), F32),
        ],
        compiler_params=_cparams(("arbitrary", "arbitrary")),
        name="ffn_up",
    )(h, h, h, mod, g, wa, wg, cwa, cwg, cba, cbg)


def _rope_tables(n_lat, n_ctx, dim):
    n_freq = dim // 4
    inv_freq = ROPE_THETA ** (-jnp.arange(n_freq, dtype=F32) / n_freq)
    t = jnp.arange(n_lat, dtype=jnp.int32)
    pos = jnp.stack([t // GRID_W, t % GRID_W], axis=-1).astype(F32)
    ang = pos[:, :, None] * inv_freq
    cos, sin = jnp.cos(ang), jnp.sin(ang)
    cos_l = jnp.concatenate([cos[:, 0], cos[:, 0], cos[:, 1], cos[:, 1]], axis=-1)
    sin_l = jnp.concatenate([-sin[:, 0], sin[:, 0], -sin[:, 1], sin[:, 1]], axis=-1)
    rep = HEAD // dim
    cos_l = jnp.tile(cos_l, (1, rep))
    sin_l = jnp.tile(sin_l, (1, rep))
    cos_l = jnp.concatenate([cos_l, jnp.ones((n_ctx, HEAD), F32)], axis=0)
    sin_l = jnp.concatenate([sin_l, jnp.zeros((n_ctx, HEAD), F32)], axis=0)
    return cos_l, sin_l


def kernel(x, c, ctx, c_ctx, w_mod, b_mod, g_norm1, w_in, gq_a, gk_a, gq_b, gk_b, rpb_b, gq_c, gk_c, lam_q1, lam_k1, lam_q2, lam_k2, g_subln_c, w_br_a, w_br_b, w_br_c, w_o, g_norm2, w_up, conv_w, conv_b, w_down):
    batch, n_lat, d = x.shape
    n_ctx = ctx.shape[1]
    depth = w_mod.shape[0]
    d_ff = w_down.shape[1]
    assert batch == 1 and d == 16 * HEAD
    assert (n_lat + n_ctx) % BM == 0 and n_lat % BK_GQA == 0 and n_lat % BQ == 0 and n_ctx == BQ
    ff_pad = -(-d_ff // BN) * BN

    wqkv = w_in[:, :, :N_QKV].astype(BF16)
    wgate = jnp.transpose(w_in[:, :, N_QKV:].reshape(depth, d, 3, d), (0, 2, 1, 3)).astype(BF16)
    wbra, wbrb, wbrc, wo = (w.astype(BF16) for w in (w_br_a, w_br_b, w_br_c, w_o))
    padc = ((0, 0), (0, 0), (0, ff_pad - d_ff))
    wupa = jnp.pad(w_up[:, :, :d_ff].astype(BF16), padc)
    wupg = jnp.pad(w_up[:, :, d_ff:].astype(BF16), padc)
    cwa = jnp.pad(conv_w[:, :, :d_ff], padc)
    cwg = jnp.pad(conv_w[:, :, d_ff:], padc)
    cba = jnp.pad(conv_b[:, None, :d_ff], padc)
    cbg = jnp.pad(conv_b[:, None, d_ff:], padc)
    wdown = jnp.pad(w_down.astype(BF16), ((0, 0), (0, ff_pad - d_ff), (0, 0)))

    ones = jnp.ones((depth, HEAD), F32)
    gains = jnp.stack([gq_a, gk_a, gq_b, gk_b, jnp.tile(gq_c, (1, 2)), jnp.tile(gk_c, (1, 2)), ones, ones], axis=1)
    pad64 = lambda v: jnp.pad(v, ((0, 0), (0, HEAD - C_HALF)))
    lam_init = jnp.asarray([0.8 - 0.6 * math.exp(-0.3 * l) for l in range(depth)], F32)
    cpar = jnp.stack([pad64(lam_q1), pad64(lam_k1), pad64(lam_q2), pad64(lam_k2), g_subln_c,
                      jnp.broadcast_to(lam_init[:, None], (depth, HEAD)), ones, ones], axis=1)
    apar = jnp.zeros((8, HEAD), F32)

    rope = _rope_tables(n_lat, n_ctx, HEAD) + _rope_tables(n_lat, n_ctx, C_HALF)
    cc = jnp.zeros((8, d), F32).at[0].set(c[0]).at[1].set(c_ctx)

    mod = _modulation(cc, w_mod, b_mod)
    bias = _nbr_bias_tiles(_rpb_blocks(rpb_b), n_lat // GRID_W)

    h = jnp.concatenate([x[0], ctx[0]], axis=0)
    att = dict(n_lat=n_lat, n_ctx=n_ctx)
    for l in range(depth):
        g1 = g_norm1[l][None]
        qkv = _qkv_proj(h, mod[l], g1, wqkv[l], gains[l], rope, n_lat)
        oa = _flash(qkv, apar, n_kv_heads=A_KV, heads=1, groups=A_HEADS // A_KV, diff=False,
                    bk=BK_GQA, q0=QA0, k0=KA0, v0=VA0, **att)
        ob = _nbr_attention(qkv, bias, l, **att)
        oc = _flash(qkv, cpar[l], n_kv_heads=C_HEADS, heads=2, groups=2, diff=True,
                    bk=BK_DIFF, q0=QC0, k0=KC0, v0=VC0, **att)
        merged = _merge(h, mod[l], g1, oa, ob, oc, wgate[l], wbra[l], wbrb[l], wbrc[l], n_lat)
        h = _proj_res(merged, wo[l], h, mod[l], 2 * d, n_lat, "out_proj")
        act = _ffn_up(h, mod[l], g_norm2[l][None], wupa[l], wupg[l], cwa[l], cwg[l], cba[l], cbg[l], n_lat)
        h = _proj_res(act, wdown[l], h, mod[l], 5 * d, n_lat, "ffn_down")
    return h[:n_lat][None]
```

```python
import functools
import math

import numpy as np
import jax
import jax.numpy as jnp
from jax import lax
from jax.experimental import pallas as pl
from jax.experimental.pallas import tpu as pltpu

F32 = jnp.float32
BF16 = jnp.bfloat16

HEAD = 128
HALF = HEAD // 2
GRID_W = 64
A_HEADS, A_KV = 8, 2
B_HEADS = 4
C_HEADS = 4
C_HALF = 64
NA_KH, NA_KW = 8, 16
ROPE_THETA = 10000.0
EPS = 1e-6
NEG = -1e30
LOG2E = math.log2(math.e)
Q_SCALE_AB = HEAD ** -0.5 * LOG2E
Q_SCALE_C = C_HALF ** -0.5 * LOG2E

QA0, KA0, VA0 = 0, 8, 10
QB0, KB0, VB0 = 12, 16, 20
QC0, KC0, VC0 = 24, 28, 32
N_QKV = 36 * HEAD

VMEM_LIMIT = 56 * 1024 * 1024
MXU_DIM = 256
BM = 768
BN = 512
BN_MERGE = 256
FFN_SUB = 256
HALO = 16
NORM_ROWS = 16
BQ = 256
BK_GQA = 4096
BK_DIFF = 2048
NBR_QROWS = BQ // GRID_W
NBR_KROWS = NBR_QROWS + NA_KH
FAST_BOUND = 48.0


def _cparams(sem):
    return pltpu.CompilerParams(dimension_semantics=sem, vmem_limit_bytes=VMEM_LIMIT)


def _sigmoid(x):
    return 1.0 / (1.0 + jnp.exp(-x))


def _dot(a, b):
    return jnp.dot(a, b, preferred_element_type=F32)


def _dot_nt(a, b):
    return lax.dot_general(a, b, (((1,), (1,)), ((), ())), preferred_element_type=F32)


def _layer_spec(block, index_map, layer):
    return pl.BlockSpec((None,) + tuple(block), lambda *g: (layer,) + tuple(index_map(*g)))


def _mod_kernel(cc_ref, w_ref, b_ref, o_ref):
    cc = cc_ref[...]
    a = (cc * _sigmoid(cc)).astype(BF16)
    o_ref[...] = _dot(a, w_ref[...].astype(BF16)) + b_ref[...]


def _modulation(cc, w_mod, b_mod):
    depth, d, n = w_mod.shape
    bn = 1024
    return pl.pallas_call(
        _mod_kernel,
        out_shape=jax.ShapeDtypeStruct((depth, 8, n), F32),
        grid=(depth, n // bn),
        in_specs=[
            pl.BlockSpec((8, d), lambda l, j: (0, 0)),
            pl.BlockSpec((None, d, bn), lambda l, j: (l, 0, j)),
            pl.BlockSpec((None, 1, bn), lambda l, j: (l, 0, j)),
        ],
        out_specs=pl.BlockSpec((None, 8, bn), lambda l, j: (l, 0, j)),
        compiler_params=_cparams(("arbitrary", "arbitrary")),
        name="adaln_mod",
    )(cc, w_mod, b_mod.reshape(depth, 1, n))


def _norm_modulate(h_ref, u_ref, u_row0, g_ref, mod_ref, col0, row0, n_lat):
    n, d = h_ref.shape
    g = g_ref[...]

    def body(c, carry):
        r = pl.multiple_of(c * NORM_ROWS, NORM_ROWS)
        x = h_ref[pl.ds(r, NORM_ROWS), :]
        y = x * lax.rsqrt(jnp.mean(x * x, axis=-1, keepdims=True) + EPS) * g
        who = (row0 + r >= n_lat).astype(jnp.int32)
        shift = mod_ref[pl.ds(who, 1), col0:col0 + d]
        scale = mod_ref[pl.ds(who, 1), col0 + d:col0 + 2 * d]
        dst = pl.multiple_of(u_row0 + r, NORM_ROWS)
        u_ref[pl.ds(dst, NORM_ROWS), :] = (y * (1.0 + scale) + shift).astype(BF16)
        return carry

    trips = n // NORM_ROWS
    lax.fori_loop(0, trips, body, 0, unroll=math.gcd(trips, 4))


def _diff_lanes(shape):
    return (lax.broadcasted_iota(jnp.int32, shape, 1) % HALF) < HALF // 2


def _rms_heads(x, gain, two_components):
    x2 = x * x
    if not two_components:
        ms = jnp.mean(x2, axis=-1, keepdims=True)
    else:
        first = _diff_lanes(x.shape)
        s0 = jnp.sum(jnp.where(first, x2, 0.0), axis=-1, keepdims=True)
        s1 = jnp.sum(jnp.where(first, 0.0, x2), axis=-1, keepdims=True)
        ms = jnp.where(first, s0, s1) * (1.0 / C_HALF)
    return x * lax.rsqrt(ms + EPS) * gain


def _rope(x, cos, sin_signed):
    return x * cos + pltpu.roll(x, HALF, 1) * sin_signed


_RAW = ("raw", 0, 1.0)
_QKV_BLOCKS = (
    (("rope_a", 0, Q_SCALE_AB),) * 4, (("rope_a", 0, Q_SCALE_AB),) * 4,
    (("rope_a", 1, 1.0), ("rope_a", 1, 1.0), _RAW, _RAW),
    (("norm", 2, Q_SCALE_AB),) * 4, (("norm", 3, 1.0),) * 4, (_RAW,) * 4,
    (("rope_c", 4, Q_SCALE_C),) * 4, (("rope_c", 5, 1.0),) * 4, (_RAW,) * 4,
)


def _qkv_kernel(h_ref, mod_ref, g_ref, w_ref, gains_ref, ca_ref, sa_ref, cc_ref, sc_ref,
                o_ref, u_ref, *, n_lat):
    i, j = pl.program_id(0), pl.program_id(1)
    bm = h_ref.shape[0]

    @pl.when(j == 0)
    def _():
        _norm_modulate(h_ref, u_ref, 0, g_ref, mod_ref, 0, i * bm, n_lat)

    acc = _dot(u_ref[...], w_ref[...])

    def epilogue(heads):
        for k, (kind, gr, out_scale) in enumerate(heads):
            x = acc[:, k * HEAD:(k + 1) * HEAD]
            gain = gains_ref[gr:gr + 1, :] * out_scale
            if kind == "rope_a":
                x = _rope(_rms_heads(x, gain, False), ca_ref[...], sa_ref[...])
            elif kind == "norm":
                x = _rms_heads(x, gain, False)
            elif kind == "rope_c":
                x = _rope(_rms_heads(x, gain, True), cc_ref[...], sc_ref[...])
            o_ref[:, k * HEAD:(k + 1) * HEAD] = x.astype(BF16)

    for heads in sorted(set(_QKV_BLOCKS)):
        cond = None
        for jj, hh in enumerate(_QKV_BLOCKS):
            if hh == heads:
                cond = (j == jj) if cond is None else (cond | (j == jj))
        pl.when(cond)(functools.partial(epilogue, heads))


def _qkv_proj(h, mod, g, w, gains, rope, layer, n_lat):
    ntok, d = h.shape
    n = w.shape[-1]
    ca, sa, cc, sc = rope
    return pl.pallas_call(
        functools.partial(_qkv_kernel, n_lat=n_lat),
        out_shape=(jax.ShapeDtypeStruct((ntok, n), BF16), jax.ShapeDtypeStruct((ntok, d), BF16)),
        grid=(ntok // BM, n // BN),
        in_specs=[
            pl.BlockSpec((BM, d), lambda i, j: (i, 0)),
            _layer_spec(mod.shape[1:], lambda i, j: (0, 0), layer),
            _layer_spec((1, d), lambda i, j: (0, 0), layer),
            _layer_spec((d, BN), lambda i, j: (0, j), layer),
            _layer_spec(gains.shape[1:], lambda i, j: (0, 0), layer),
            pl.BlockSpec((BM, HEAD), lambda i, j: (i, 0)),
            pl.BlockSpec((BM, HEAD), lambda i, j: (i, 0)),
            pl.BlockSpec((BM, HEAD), lambda i, j: (i, 0)),
            pl.BlockSpec((BM, HEAD), lambda i, j: (i, 0)),
        ],
        out_specs=(pl.BlockSpec((BM, BN), lambda i, j: (i, j)),
                   pl.BlockSpec((BM, d), lambda i, j: (i, 0))),
        compiler_params=_cparams(("arbitrary", "arbitrary")),
        name="qkv_proj",
    )(h, mod, g, w, gains, ca, sa, cc, sc)


def _flash_kernel(q_ref, k_ref, v_ref, par_ref, o_ref,
                  ka_ref, va_ref, kmax_ref, qa_ref, acc_ref, m_ref, l_ref, acc2_ref,
                  *, heads, groups, diff, bk, n_lat, n_ctx):
    i = pl.program_id(1)
    bq = q_ref.shape[0]
    ntok = k_ref.shape[0]
    qw = q_ref.shape[1] // heads
    is_lat = i * bq < n_lat
    n_main = jnp.where(is_lat, n_lat // bk - 1, 0)
    last0 = n_lat - bk

    def lane_sets(shape):
        if not diff:
            return [None] * groups
        first = _diff_lanes(shape)
        return [first, jnp.logical_not(first)]

    @pl.when(i == 0)
    def _():
        unit = (lax.broadcasted_iota(jnp.int32, (ntok, HEAD), 1) == 0).astype(F32).astype(BF16)
        for h in range(heads):
            k = k_ref[:, h * HEAD:(h + 1) * HEAD]
            ka_ref[h, :, :HEAD] = k
            ka_ref[h, :, HEAD:] = unit
            va_ref[h, :, :HEAD] = v_ref[:, h * HEAD:(h + 1) * HEAD]
            va_ref[h, :, HEAD:] = unit
            k2 = k.astype(F32)
            k2 = k2 * k2
            for g, lanes in enumerate(lane_sets(k2.shape)[:2 if diff else 1]):
                kk = k2 if lanes is None else jnp.where(lanes, k2, 0.0)
                kmax_ref[2 * h + g] = jnp.max(jnp.sum(kk, axis=-1, keepdims=True), axis=0, keepdims=True)

    lane0 = lax.broadcasted_iota(jnp.int32, (bq, HEAD), 1) == 0
    bound = None
    for h in range(heads):
        for g, lanes in enumerate(lane_sets((bq, HEAD))):
            if diff:
                qv = q_ref[:, h * qw:(h + 1) * qw]
                qg = jnp.where(lanes, qv, jnp.zeros_like(qv))
            else:
                qg = q_ref[:, h * qw + g * HEAD:h * qw + (g + 1) * HEAD]
            qf = qg.astype(F32)
            b = jnp.sqrt(jnp.sum(qf * qf, axis=-1, keepdims=True) * kmax_ref[2 * h + (g if diff else 0)])
            qa_ref[h, g * bq:(g + 1) * bq, :HEAD] = qg
            qa_ref[h, g * bq:(g + 1) * bq, HEAD:] = jnp.where(lane0, -b, 0.0).astype(BF16)
            bound = b if bound is None else jnp.maximum(bound, b)
    fast = jnp.max(bound) <= FAST_BOUND

    def finalize(get):
        for h in range(heads):
            if diff:
                lam = (jnp.exp(jnp.sum(par_ref[0:1, :] * par_ref[1:2, :], axis=-1, keepdims=True))
                       - jnp.exp(jnp.sum(par_ref[2:3, :] * par_ref[3:4, :], axis=-1, keepdims=True))
                       + par_ref[5:6, 0:1])
                (a0, l0), (a1, l1) = get(h, 0), get(h, 1)
                o = a0 / l0 - lam * (a1 / l1)
                o = o * lax.rsqrt(jnp.mean(o * o, axis=-1, keepdims=True) + EPS) * par_ref[4:5, :]
                o_ref[:, h * qw:(h + 1) * qw] = (o * (1.0 - par_ref[5:6, 0:1])).astype(BF16)
            else:
                for g in range(groups):
                    a, l = get(h, g)
                    o_ref[:, h * qw + g * HEAD:h * qw + (g + 1) * HEAD] = (a / l).astype(BF16)

    def sweep(chunk, kr, vr):
        def body(c, carry):
            sl = pl.ds(pl.multiple_of(c * bk, bk), bk)
            chunk(lambda h: kr(h, sl), lambda h: vr(h, sl))
            return carry

        lax.fori_loop(0, n_main, body, 0)

        @pl.when(is_lat)
        def _():
            sl = slice(last0, n_lat + n_ctx)
            chunk(lambda h: kr(h, sl), lambda h: vr(h, sl))

        @pl.when(jnp.logical_not(is_lat))
        def _():
            sl = slice(n_lat, n_lat + n_ctx)
            chunk(lambda h: kr(h, sl), lambda h: vr(h, sl))

    @pl.when(fast)
    def _():
        acc_ref[...] = jnp.zeros(acc_ref.shape, F32)

        def chunk(k, v):
            for h in range(heads):
                p = jnp.exp2(_dot_nt(qa_ref[h], k(h))).astype(BF16)
                acc_ref[h] += _dot(p, v(h))

        sweep(chunk, lambda h, sl: ka_ref[h, sl, :], lambda h, sl: va_ref[h, sl, :])
        finalize(lambda h, g: (acc_ref[h, g * bq:(g + 1) * bq, :HEAD],
                               acc_ref[h, g * bq:(g + 1) * bq, HEAD:HEAD + 1]))

    @pl.when(jnp.logical_not(fast))
    def _():
        m_ref[...] = jnp.full(m_ref.shape, NEG, F32)
        l_ref[...] = jnp.zeros(l_ref.shape, F32)
        acc2_ref[...] = jnp.zeros(acc2_ref.shape, F32)

        def chunk(k, v):
            for h in range(heads):
                for g in range(groups):
                    r = h * groups + g
                    s = _dot_nt(qa_ref[h, g * bq:(g + 1) * bq, :HEAD], k(h))
                    m_prev = m_ref[r]
                    m_new = jnp.maximum(m_prev, jnp.max(s, axis=-1, keepdims=True))
                    alpha = jnp.exp2(m_prev - m_new)
                    p = jnp.exp2(s - m_new)
                    l_ref[r] = alpha * l_ref[r] + jnp.sum(p, axis=-1, keepdims=True)
                    acc2_ref[r] = alpha * acc2_ref[r] + _dot(p.astype(BF16), v(h))
                    m_ref[r] = m_new

        sweep(chunk, lambda h, sl: k_ref[sl, h * HEAD:(h + 1) * HEAD],
              lambda h, sl: v_ref[sl, h * HEAD:(h + 1) * HEAD])
        finalize(lambda h, g: (acc2_ref[h * groups + g], l_ref[h * groups + g]))


def _flash(qkv, par, layer, *, n_kv_heads, heads, groups, diff, bq, bk, q0, k0, v0, n_lat, n_ctx):
    ntok = qkv.shape[0]
    qw = heads * (HEAD if diff else groups * HEAD)
    kw = heads * HEAD
    assert n_lat % bq == 0 and n_ctx % bq == 0 and n_lat % bk == 0
    return pl.pallas_call(
        functools.partial(_flash_kernel, heads=heads, groups=groups, diff=diff, bk=bk,
                          n_lat=n_lat, n_ctx=n_ctx),
        out_shape=jax.ShapeDtypeStruct((ntok, n_kv_heads // heads * qw), BF16),
        grid=(n_kv_heads // heads, ntok // bq),
        in_specs=[
            pl.BlockSpec((bq, qw), lambda h, i: (i, q0 * HEAD // qw + h)),
            pl.BlockSpec((ntok, kw), lambda h, i: (0, k0 * HEAD // kw + h)),
            pl.BlockSpec((ntok, kw), lambda h, i: (0, v0 * HEAD // kw + h)),
            _layer_spec(par.shape[1:], lambda h, i: (0, 0), layer),
        ],
        out_specs=pl.BlockSpec((bq, qw), lambda h, i: (i, h)),
        scratch_shapes=[
            pltpu.VMEM((heads, ntok, MXU_DIM), BF16),
            pltpu.VMEM((heads, ntok, MXU_DIM), BF16),
            pltpu.VMEM((2 * heads, 1, 1), F32),
            pltpu.VMEM((heads, groups * bq, MXU_DIM), BF16),
            pltpu.VMEM((heads, groups * bq, MXU_DIM), F32),
            pltpu.VMEM((heads * groups, bq, 1), F32),
            pltpu.VMEM((heads * groups, bq, 1), F32),
            pltpu.VMEM((heads * groups, bq, HEAD), F32),
        ],
        compiler_params=_cparams(("arbitrary", "arbitrary")),
        name="flash_diff" if diff else "flash_gqa",
    )(qkv, qkv, qkv, par)


def _rpb_blocks_kernel(rpb_ref, o_ref):
    n_dr, n_dc = 2 * NA_KH - 1, 2 * NA_KW - 1
    base = pl.program_id(0) * (n_dr * n_dc)
    qc = lax.broadcasted_iota(jnp.int32, (GRID_W, GRID_W), 0)
    kc = lax.broadcasted_iota(jnp.int32, (GRID_W, GRID_W), 1)
    cs = jnp.clip(qc - NA_KW // 2, 0, GRID_W - NA_KW)
    in_win = (kc >= cs) & (kc < cs + NA_KW)
    dc = kc - qc + (NA_KW - 1)
    for a in range(n_dr):
        blk = jnp.zeros((GRID_W, GRID_W), F32)
        for b in range(n_dc):
            blk = jnp.where(dc == b, rpb_ref[base + a * n_dc + b], blk)
        o_ref[a] = jnp.where(in_win, blk * LOG2E, NEG)


def _rpb_blocks(rpb):
    lh = rpb.shape[0] * rpb.shape[1]
    n_dr = 2 * NA_KH - 1
    return pl.pallas_call(
        _rpb_blocks_kernel,
        out_shape=jax.ShapeDtypeStruct((lh, n_dr, GRID_W, GRID_W), F32),
        grid=(lh,),
        in_specs=[pl.BlockSpec(memory_space=pltpu.SMEM)],
        out_specs=pl.BlockSpec((None, n_dr, GRID_W, GRID_W), lambda p: (p, 0, 0, 0)),
        compiler_params=_cparams(("arbitrary",)),
        name="rpb_blocks",
    )(rpb.reshape(-1))


def _nbr_window_plan(rows):
    nblk = rows // NBR_QROWS

    def plan(blk):
        r0 = blk * NBR_QROWS
        k0 = min(max(r0 - NA_KH // 2, 0), rows - NBR_KROWS)
        dr = np.zeros((NBR_QROWS, NBR_KROWS), np.int32)
        ok = np.zeros((NBR_QROWS, NBR_KROWS), bool)
        for qr in range(NBR_QROWS):
            r = r0 + qr
            rs = min(max(r - NA_KH // 2, 0), rows - NA_KH)
            for kr in range(NBR_KROWS):
                ok[qr, kr] = rs <= k0 + kr < rs + NA_KH
                dr[qr, kr] = min(max(k0 + kr - r + NA_KH - 1, 0), 2 * NA_KH - 2)
        return dr, ok

    classes = [plan(0), plan(1), plan(nblk - 1)]
    for blk in range(1, nblk - 1):
        dr, ok = plan(blk)
        assert (dr[ok] == classes[1][0][ok]).all() and (ok == classes[1][1]).all()
    return np.stack([c[0] for c in classes]), np.stack([c[1] for c in classes])


def _nbr_bias_tiles(blocks, rows):
    dr, ok = _nbr_window_plan(rows)
    t = blocks[:, dr]
    t = jnp.where(jnp.asarray(ok)[None, :, :, :, None, None], t, NEG)
    t = jnp.transpose(t, (0, 1, 2, 4, 3, 5))
    return t.reshape(blocks.shape[0], 3, BQ, NBR_KROWS * GRID_W)


def _nbr_kernel(q_ref, k_ref, v_ref, bias_ref, o_ref, *, n_lat, n_ctx):
    i = pl.program_id(1)
    bq = q_ref.shape[0]
    rows = n_lat // GRID_W
    q = q_ref[...]
    k_ctx = k_ref[n_lat:n_lat + n_ctx, :]
    v_ctx = v_ref[n_lat:n_lat + n_ctx, :]
    s_ctx = _dot_nt(q, k_ctx)
    m_ctx = jnp.max(s_ctx, axis=-1, keepdims=True)

    @pl.when(i * bq < n_lat)
    def _():
        k0 = jnp.clip(i * NBR_QROWS - NA_KH // 2, 0, rows - NBR_KROWS)
        start = pl.multiple_of(k0 * GRID_W, NBR_QROWS * GRID_W)
        nk = NBR_KROWS * GRID_W
        s_win = _dot_nt(q, k_ref[pl.ds(start, nk), :]) + bias_ref[...]
        m = jnp.maximum(m_ctx, jnp.max(s_win, axis=-1, keepdims=True))
        p_win = jnp.exp2(s_win - m)
        p_ctx = jnp.exp2(s_ctx - m)
        l = jnp.sum(p_win, axis=-1, keepdims=True) + jnp.sum(p_ctx, axis=-1, keepdims=True)
        o = _dot(p_win.astype(BF16), v_ref[pl.ds(start, nk), :]) + _dot(p_ctx.astype(BF16), v_ctx)
        o_ref[...] = (o / l).astype(BF16)

    @pl.when(i * bq >= n_lat)
    def _():
        p_ctx = jnp.exp2(s_ctx - m_ctx)
        l = jnp.sum(p_ctx, axis=-1, keepdims=True)
        o_ref[...] = (_dot(p_ctx.astype(BF16), v_ctx) / l).astype(BF16)


def _nbr_attention(qkv, bias, layer, *, n_lat, n_ctx):
    ntok = qkv.shape[0]
    n_lat_blocks = n_lat // BQ
    nk = NBR_KROWS * GRID_W

    def bias_map(h, i):
        cls = jnp.where(i == 0, 0, jnp.where(i == n_lat_blocks - 1, 2, 1))
        return (layer * B_HEADS + h, cls, 0, 0)

    return pl.pallas_call(
        functools.partial(_nbr_kernel, n_lat=n_lat, n_ctx=n_ctx),
        out_shape=jax.ShapeDtypeStruct((ntok, B_HEADS * HEAD), BF16),
        grid=(B_HEADS, ntok // BQ),
        in_specs=[
            pl.BlockSpec((BQ, HEAD), lambda h, i: (i, QB0 + h)),
            pl.BlockSpec((ntok, HEAD), lambda h, i: (0, KB0 + h)),
            pl.BlockSpec((ntok, HEAD), lambda h, i: (0, VB0 + h)),
            pl.BlockSpec((None, None, BQ, nk), bias_map),
        ],
        out_specs=pl.BlockSpec((BQ, HEAD), lambda h, i: (i, h)),
        compiler_params=_cparams(("arbitrary", "arbitrary")),
        name="nbr_attn",
    )(qkv, qkv, qkv, bias)


def _merge_kernel(u_ref, oa_ref, ob_ref, oc_ref, wg_ref, wa_ref, wb_ref, wc_ref, o_ref):
    u = u_ref[...]
    merged = _sigmoid(_dot(u, wg_ref[0])) * _dot(oa_ref[...], wa_ref[...])
    merged += _sigmoid(_dot(u, wg_ref[1])) * _dot(ob_ref[...], wb_ref[...])
    merged += _sigmoid(_dot(u, wg_ref[2])) * _dot(oc_ref[...], wc_ref[...])
    o_ref[...] = merged.astype(BF16)


def _merge(u, oa, ob, oc, wg, wa, wb, wc, layer):
    ntok, d = u.shape
    return pl.pallas_call(
        _merge_kernel,
        out_shape=jax.ShapeDtypeStruct((ntok, d), BF16),
        grid=(ntok // BM, d // BN_MERGE),
        in_specs=[
            pl.BlockSpec((BM, d), lambda i, j: (i, 0)),
            pl.BlockSpec((BM, oa.shape[1]), lambda i, j: (i, 0)),
            pl.BlockSpec((BM, ob.shape[1]), lambda i, j: (i, 0)),
            pl.BlockSpec((BM, oc.shape[1]), lambda i, j: (i, 0)),
            _layer_spec((3, d, BN_MERGE), lambda i, j: (0, 0, j), layer),
            _layer_spec((wa.shape[1], BN_MERGE), lambda i, j: (0, j), layer),
            _layer_spec((wb.shape[1], BN_MERGE), lambda i, j: (0, j), layer),
            _layer_spec((wc.shape[1], BN_MERGE), lambda i, j: (0, j), layer),
        ],
        out_specs=pl.BlockSpec((BM, BN_MERGE), lambda i, j: (i, j)),
        compiler_params=_cparams(("arbitrary", "arbitrary")),
        name="branch_merge",
    )(u, oa, ob, oc, wg, wa, wb, wc)


def _proj_res_kernel(a_ref, w_ref, h_ref, gate_ref, o_ref, *, n_lat):
    bm = h_ref.shape[0]
    rows = pl.program_id(0) * bm + lax.broadcasted_iota(jnp.int32, (bm, 1), 0)
    gate = jnp.where(rows >= n_lat, gate_ref[1:2, :], gate_ref[0:1, :])
    o_ref[...] = h_ref[...] + gate * _dot(a_ref[...], w_ref[...])


def _proj_res(a, w, h, mod, gate_col, layer, n_lat, name):
    ntok, d = h.shape
    kdim = a.shape[1]
    return pl.pallas_call(
        functools.partial(_proj_res_kernel, n_lat=n_lat),
        out_shape=jax.ShapeDtypeStruct((ntok, d), F32),
        grid=(ntok // BM, d // BN),
        in_specs=[
            pl.BlockSpec((BM, kdim), lambda i, j: (i, 0)),
            _layer_spec((kdim, BN), lambda i, j: (0, j), layer),
            pl.BlockSpec((BM, BN), lambda i, j: (i, j)),
            _layer_spec((8, BN), lambda i, j: (0, gate_col // BN + j), layer),
        ],
        out_specs=pl.BlockSpec((BM, BN), lambda i, j: (i, j)),
        compiler_params=_cparams(("arbitrary", "arbitrary")),
        name=name,
    )(a, w, h, mod)


def _ffn_up_kernel(h_ref, hp_ref, hn_ref, mod_ref, g_ref, wa_ref, wg_ref, cwa_ref, cwg_ref,
                   cba_ref, cbg_ref, o_ref, u_ref, ya_ref, yg_ref, *, n_lat, n_tok):
    i, j = pl.program_id(0), pl.program_id(1)
    bm = h_ref.shape[0]
    d = h_ref.shape[1]
    col = 3 * d
    row0 = i * bm

    @pl.when(j == 0)
    def _():
        _norm_modulate(hp_ref, u_ref, 0, g_ref, mod_ref, col, row0 - HALO, n_lat)
        _norm_modulate(h_ref, u_ref, HALO, g_ref, mod_ref, col, row0, n_lat)
        _norm_modulate(hn_ref, u_ref, HALO + bm, g_ref, mod_ref, col, row0 + bm, n_lat)

    rows = row0 + lax.broadcasted_iota(jnp.int32, (bm, 1), 0)
    has_prev = ((rows != 0) & (rows != n_lat)).astype(F32)
    has_next = ((rows != n_lat - 1) & (rows != n_tok - 1)).astype(F32)

    def conv(y_ref, cw_ref, cb_ref, cols):
        prev = y_ref[pl.ds(HALO - 1, bm), cols] * has_prev
        cur = y_ref[pl.ds(HALO, bm), cols]
        nxt = y_ref[pl.ds(HALO + 1, bm), cols] * has_next
        return cb_ref[:, cols] + prev * cw_ref[0:1, cols] + cur * cw_ref[1:2, cols] + nxt * cw_ref[2:3, cols]

    u = u_ref[...]
    for s in range(o_ref.shape[1] // FFN_SUB):
        cols = slice(s * FFN_SUB, (s + 1) * FFN_SUB)
        ya_ref[:, cols] = _dot(u, wa_ref[:, cols])
        yg_ref[:, cols] = _dot(u, wg_ref[:, cols])
        a = conv(ya_ref, cwa_ref, cba_ref, cols)
        gt = conv(yg_ref, cwg_ref, cbg_ref, cols)
        o_ref[:, cols] = (gt * _sigmoid(gt) * a).astype(BF16)


def _ffn_up(h, mod, g, wa, wg, cwa, cwg, cba, cbg, layer, n_lat):
    ntok, d = h.shape
    ff = wa.shape[-1]
    per = BM // HALO
    last = ntok // HALO - 1
    return pl.pallas_call(
        functools.partial(_ffn_up_kernel, n_lat=n_lat, n_tok=ntok),
        out_shape=jax.ShapeDtypeStruct((ntok, ff), BF16),
        grid=(ntok // BM, ff // BN),
        in_specs=[
            pl.BlockSpec((BM, d), lambda i, j: (i, 0)),
            pl.BlockSpec((HALO, d), lambda i, j: (jnp.maximum(i * per - 1, 0), 0)),
            pl.BlockSpec((HALO, d), lambda i, j: (jnp.minimum((i + 1) * per, last), 0)),
            _layer_spec(mod.shape[1:], lambda i, j: (0, 0), layer),
            _layer_spec((1, d), lambda i, j: (0, 0), layer),
            _layer_spec((d, BN), lambda i, j: (0, j), layer),
            _layer_spec((d, BN), lambda i, j: (0, j), layer),
            _layer_spec((3, BN), lambda i, j: (0, j), layer),
            _layer_spec((3, BN), lambda i, j: (0, j), layer),
            _layer_spec((1, BN), lambda i, j: (0, j), layer),
            _layer_spec((1, BN), lambda i, j: (0, j), layer),
        ],
        out_specs=pl.BlockSpec((BM, BN), lambda i, j: (i, j)),
        scratch_shapes=[
            pltpu.VMEM((BM + 2 * HALO, d), BF16),
            pltpu.VMEM((BM + 2 * HALO, BN), F32),
            pltpu.VMEM((BM + 2 * HALO, BN), F32),
        ],
        compiler_params=_cparams(("arbitrary", "arbitrary")),
        name="ffn_up",
    )(h, h, h, mod, g, wa, wg, cwa, cwg, cba, cbg)


def _rot_layout_a(v):
    sh = v.shape
    return jnp.swapaxes(v.reshape(sh[:-1] + (-1, 2, 2, HEAD // 4)), -3, -2).reshape(sh)


def _rot_layout_c(v):
    sh = v.shape
    return jnp.moveaxis(v.reshape(sh[:-1] + (-1, 2, 2, 2, C_HALF // 4)), -2, -4).reshape(sh)


def _rope_tables(n_lat, n_ctx, dim):
    n_freq = dim // 4
    inv_freq = ROPE_THETA ** (-jnp.arange(n_freq, dtype=F32) / n_freq)
    t = jnp.arange(n_lat, dtype=jnp.int32)
    pos = jnp.stack([t // GRID_W, t % GRID_W], axis=-1).astype(F32)
    ang = (pos[:, :, None] * inv_freq).reshape(n_lat, 2 * n_freq)
    reps = HALF // (2 * n_freq)
    cos_h = jnp.tile(jnp.cos(ang), (1, reps))
    sin_h = jnp.tile(jnp.sin(ang), (1, reps))
    cos_l = jnp.concatenate([cos_h, cos_h], axis=-1)
    sin_l = jnp.concatenate([-sin_h, sin_h], axis=-1)
    cos_l = jnp.concatenate([cos_l, jnp.ones((n_ctx, HEAD), F32)], axis=0)
    sin_l = jnp.concatenate([sin_l, jnp.zeros((n_ctx, HEAD), F32)], axis=0)
    return cos_l, sin_l


def kernel(x, c, ctx, c_ctx, w_mod, b_mod, g_norm1, w_in, gq_a, gk_a, gq_b, gk_b, rpb_b, gq_c, gk_c, lam_q1, lam_k1, lam_q2, lam_k2, g_subln_c, w_br_a, w_br_b, w_br_c, w_o, g_norm2, w_up, conv_w, conv_b, w_down):
    batch, n_lat, d = x.shape
    n_ctx = ctx.shape[1]
    depth = w_mod.shape[0]
    d_ff = w_down.shape[1]
    assert batch == 1 and d == 16 * HEAD
    assert (n_lat + n_ctx) % BM == 0 and n_lat % BQ == 0 and n_ctx == BQ
    ff_pad = -(-d_ff // BN) * BN

    col = lambda a, b: w_in[:, :, a * HEAD:b * HEAD].astype(BF16)
    wqkv = jnp.concatenate([
        _rot_layout_a(col(QA0, KA0)), _rot_layout_a(col(KA0, VA0)), col(VA0, QC0),
        _rot_layout_c(col(QC0, KC0)), _rot_layout_c(col(KC0, VC0)), col(VC0, VC0 + C_HEADS)], axis=-1)
    wgate = jnp.transpose(w_in[:, :, N_QKV:].reshape(depth, d, 3, d), (0, 2, 1, 3)).astype(BF16)
    wbra, wbrb, wbrc, wo = (w.astype(BF16) for w in (w_br_a, w_br_b, w_br_c, w_o))
    padc = ((0, 0), (0, 0), (0, ff_pad - d_ff))
    wupa = jnp.pad(w_up[:, :, :d_ff].astype(BF16), padc)
    wupg = jnp.pad(w_up[:, :, d_ff:].astype(BF16), padc)
    cwa = jnp.pad(conv_w[:, :, :d_ff], padc)
    cwg = jnp.pad(conv_w[:, :, d_ff:], padc)
    cba = jnp.pad(conv_b[:, None, :d_ff], padc)
    cbg = jnp.pad(conv_b[:, None, d_ff:], padc)
    wdown = jnp.pad(w_down.astype(BF16), ((0, 0), (0, ff_pad - d_ff), (0, 0)))

    ones = jnp.ones((depth, HEAD), F32)
    gains = jnp.stack([_rot_layout_a(gq_a), _rot_layout_a(gk_a), gq_b, gk_b,
                       _rot_layout_c(jnp.tile(gq_c, (1, 2))), _rot_layout_c(jnp.tile(gk_c, (1, 2))),
                       ones, ones], axis=1)
    pad64 = lambda v: jnp.pad(v, ((0, 0), (0, HEAD - C_HALF)))
    lam_init = jnp.asarray([0.8 - 0.6 * math.exp(-0.3 * l) for l in range(depth)], F32)
    cpar = jnp.stack([pad64(lam_q1), pad64(lam_k1), pad64(lam_q2), pad64(lam_k2), g_subln_c,
                      jnp.broadcast_to(lam_init[:, None], (depth, HEAD)), ones, ones], axis=1)
    g1 = g_norm1[:, None, :]
    g2 = g_norm2[:, None, :]

    rope = _rope_tables(n_lat, n_ctx, HEAD) + _rope_tables(n_lat, n_ctx, C_HALF)
    cc = jnp.zeros((8, d), F32).at[0].set(c[0]).at[1].set(c_ctx)

    mod = _modulation(cc, w_mod, b_mod)
    bias = _nbr_bias_tiles(_rpb_blocks(rpb_b), n_lat // GRID_W)

    h = jnp.concatenate([x[0], ctx[0]], axis=0)
    att = dict(n_lat=n_lat, n_ctx=n_ctx)
    for l in range(depth):
        qkv, u = _qkv_proj(h, mod, g1, wqkv, gains, rope, l, n_lat)
        oa = _flash(qkv, cpar, l, n_kv_heads=A_KV, heads=1, groups=A_HEADS // A_KV, diff=False,
                    bq=BQ, bk=min(BK_GQA, n_lat), q0=QA0, k0=KA0, v0=VA0, **att)
        ob = _nbr_attention(qkv, bias, l, **att)
        oc = _flash(qkv, cpar, l, n_kv_heads=C_HEADS, heads=2, groups=2, diff=True,
                    bq=BQ, bk=min(BK_DIFF, n_lat), q0=QC0, k0=KC0, v0=VC0, **att)
        merged = _merge(u, oa, ob, oc, wgate, wbra, wbrb, wbrc, l)
        h = _proj_res(merged, wo, h, mod, 2 * d, l, n_lat, "out_proj")
        act = _ffn_up(h, mod, g2, wupa, wupg, cwa, cwg, cba, cbg, l, n_lat)
        h = _proj_res(act, wdown, h, mod, 5 * d, l, n_lat, "ffn_down")
    return h[:n_lat][None]
```

```python
import functools
import math

import numpy as np
import jax
import jax.numpy as jnp
from jax import lax
from jax.experimental import pallas as pl
from jax.experimental.pallas import tpu as pltpu

F32 = jnp.float32
BF16 = jnp.bfloat16

HEAD = 128
HALF = HEAD // 2
GRID_W = 64
A_HEADS, A_KV = 8, 2
B_HEADS = 4
C_HEADS = 4
C_HALF = 64
NA_KH, NA_KW = 8, 16
ROPE_THETA = 10000.0
EPS = 1e-6
NEG = -1e30
LOG2E = math.log2(math.e)
Q_SCALE_AB = HEAD ** -0.5 * LOG2E
Q_SCALE_C = C_HALF ** -0.5 * LOG2E

QA0, KA0, VA0 = 0, 8, 10
QB0, KB0, VB0 = 12, 16, 20
QC0, KC0, VC0 = 24, 28, 32
N_QKV = 36 * HEAD

VMEM_LIMIT = 56 * 1024 * 1024
MXU_DIM = 256
BM = 768
BN = 512
BN_MERGE = 512
FFN_SUB = 256
HALO = 16
NORM_ROWS = 16
BQ = 256
BK_GQA = 4096
BK_DIFF = 2048
NBR_HEADS = 2
NBR_QROWS = BQ // GRID_W
NBR_KROWS = NBR_QROWS + NA_KH
FAST_BOUND = 48.0


def _cparams(sem):
    return pltpu.CompilerParams(dimension_semantics=sem, vmem_limit_bytes=VMEM_LIMIT)


def _sigmoid(x):
    return 1.0 / (1.0 + jnp.exp(-x))


def _dot(a, b):
    return jnp.dot(a, b, preferred_element_type=F32)


def _dot_nt(a, b):
    return lax.dot_general(a, b, (((1,), (1,)), ((), ())), preferred_element_type=F32)


def _layer_spec(block, index_map, layer):
    return pl.BlockSpec((None,) + tuple(block), lambda *g: (layer,) + tuple(index_map(*g)))


def _mod_kernel(cc_ref, w_ref, b_ref, o_ref):
    cc = cc_ref[...]
    a = (cc * _sigmoid(cc)).astype(BF16)
    o_ref[...] = _dot(a, w_ref[...].astype(BF16)) + b_ref[...]


def _modulation(cc, w_mod, b_mod):
    depth, d, n = w_mod.shape
    bn = 1024
    return pl.pallas_call(
        _mod_kernel,
        out_shape=jax.ShapeDtypeStruct((depth, 8, n), F32),
        grid=(depth, n // bn),
        in_specs=[
            pl.BlockSpec((8, d), lambda l, j: (0, 0)),
            pl.BlockSpec((None, d, bn), lambda l, j: (l, 0, j)),
            pl.BlockSpec((None, 1, bn), lambda l, j: (l, 0, j)),
        ],
        out_specs=pl.BlockSpec((None, 8, bn), lambda l, j: (l, 0, j)),
        compiler_params=_cparams(("arbitrary", "arbitrary")),
        name="adaln_mod",
    )(cc, w_mod, b_mod.reshape(depth, 1, n))


def _norm_modulate(h_ref, u_ref, u_row0, g_ref, mod_ref, col0, row0, n_lat):
    n, d = h_ref.shape
    g = g_ref[...]

    def body(c, carry):
        r = pl.multiple_of(c * NORM_ROWS, NORM_ROWS)
        x = h_ref[pl.ds(r, NORM_ROWS), :]
        y = x * lax.rsqrt(jnp.mean(x * x, axis=-1, keepdims=True) + EPS) * g
        who = (row0 + r >= n_lat).astype(jnp.int32)
        shift = mod_ref[pl.ds(who, 1), col0:col0 + d]
        scale = mod_ref[pl.ds(who, 1), col0 + d:col0 + 2 * d]
        dst = pl.multiple_of(u_row0 + r, NORM_ROWS)
        u_ref[pl.ds(dst, NORM_ROWS), :] = (y * (1.0 + scale) + shift).astype(BF16)
        return carry

    trips = n // NORM_ROWS
    lax.fori_loop(0, trips, body, 0, unroll=math.gcd(trips, 4))


def _diff_lanes(shape):
    return (lax.broadcasted_iota(jnp.int32, shape, 1) % HALF) < HALF // 2


def _rms_heads(x, gain, two_components):
    x2 = x * x
    if not two_components:
        ms = jnp.mean(x2, axis=-1, keepdims=True)
    else:
        first = _diff_lanes(x.shape)
        s0 = jnp.sum(jnp.where(first, x2, 0.0), axis=-1, keepdims=True)
        s1 = jnp.sum(jnp.where(first, 0.0, x2), axis=-1, keepdims=True)
        ms = jnp.where(first, s0, s1) * (1.0 / C_HALF)
    return x * lax.rsqrt(ms + EPS) * gain


def _rope(x, cos, sin_signed):
    return x * cos + pltpu.roll(x, HALF, 1) * sin_signed


_RAW = ("raw", 0, 1.0)
_QKV_BLOCKS = (
    (("rope_a", 0, Q_SCALE_AB),) * 4, (("rope_a", 0, Q_SCALE_AB),) * 4,
    (("rope_a", 1, 1.0), ("rope_a", 1, 1.0), _RAW, _RAW),
    (("norm", 2, Q_SCALE_AB),) * 4, (("norm", 3, 1.0),) * 4, (_RAW,) * 4,
    (("rope_c", 4, Q_SCALE_C),) * 4, (("rope_c", 5, 1.0),) * 4, (_RAW,) * 4,
)


def _qkv_kernel(h_ref, mod_ref, g_ref, w_ref, gains_ref, ca_ref, sa_ref, cc_ref, sc_ref,
                o_ref, u_ref, *, n_lat):
    i, j = pl.program_id(0), pl.program_id(1)
    bm = h_ref.shape[0]

    @pl.when(j == 0)
    def _():
        _norm_modulate(h_ref, u_ref, 0, g_ref, mod_ref, 0, i * bm, n_lat)

    acc = _dot(u_ref[...], w_ref[...])

    def epilogue(heads):
        for k, (kind, gr, out_scale) in enumerate(heads):
            x = acc[:, k * HEAD:(k + 1) * HEAD]
            gain = gains_ref[gr:gr + 1, :] * out_scale
            if kind == "rope_a":
                x = _rope(_rms_heads(x, gain, False), ca_ref[...], sa_ref[...])
            elif kind == "norm":
                x = _rms_heads(x, gain, False)
            elif kind == "rope_c":
                x = _rope(_rms_heads(x, gain, True), cc_ref[...], sc_ref[...])
            o_ref[:, k * HEAD:(k + 1) * HEAD] = x.astype(BF16)

    for heads in sorted(set(_QKV_BLOCKS)):
        cond = None
        for jj, hh in enumerate(_QKV_BLOCKS):
            if hh == heads:
                cond = (j == jj) if cond is None else (cond | (j == jj))
        pl.when(cond)(functools.partial(epilogue, heads))


def _qkv_proj(h, mod, g, w, gains, rope, layer, n_lat):
    ntok, d = h.shape
    n = w.shape[-1]
    ca, sa, cc, sc = rope
    return pl.pallas_call(
        functools.partial(_qkv_kernel, n_lat=n_lat),
        out_shape=(jax.ShapeDtypeStruct((ntok, n), BF16), jax.ShapeDtypeStruct((ntok, d), BF16)),
        grid=(ntok // BM, n // BN),
        in_specs=[
            pl.BlockSpec((BM, d), lambda i, j: (i, 0)),
            _layer_spec(mod.shape[1:], lambda i, j: (0, 0), layer),
            _layer_spec((1, d), lambda i, j: (0, 0), layer),
            _layer_spec((d, BN), lambda i, j: (0, j), layer),
            _layer_spec(gains.shape[1:], lambda i, j: (0, 0), layer),
            pl.BlockSpec((BM, HEAD), lambda i, j: (i, 0)),
            pl.BlockSpec((BM, HEAD), lambda i, j: (i, 0)),
            pl.BlockSpec((BM, HEAD), lambda i, j: (i, 0)),
            pl.BlockSpec((BM, HEAD), lambda i, j: (i, 0)),
        ],
        out_specs=(pl.BlockSpec((BM, BN), lambda i, j: (i, j)),
                   pl.BlockSpec((BM, d), lambda i, j: (i, 0))),
        compiler_params=_cparams(("arbitrary", "arbitrary")),
        name="qkv_proj",
    )(h, mod, g, w, gains, ca, sa, cc, sc)


def _flash_kernel(q_ref, k_ref, v_ref, par_ref, o_ref,
                  ka_ref, va_ref, kmax_ref, qa_ref, acc_ref, m_ref, l_ref, acc2_ref, p_ref, lsum_ref,
                  *, heads, groups, diff, bk, n_lat, n_ctx):
    i = pl.program_id(1)
    bq = q_ref.shape[0]
    ntok = k_ref.shape[0]
    qw = q_ref.shape[1] // heads
    is_lat = i * bq < n_lat
    n_main = jnp.where(is_lat, n_lat // bk - 1, 0)
    last0 = n_lat - bk

    def lane_sets(shape):
        if not diff:
            return [None] * groups
        first = _diff_lanes(shape)
        return [first, jnp.logical_not(first)]

    @pl.when(i == 0)
    def _():
        unit = (lax.broadcasted_iota(jnp.int32, (ntok, HEAD), 1) == 0).astype(F32).astype(BF16)
        for h in range(heads):
            k = k_ref[:, h * HEAD:(h + 1) * HEAD]
            ka_ref[h, :, :HEAD] = k
            ka_ref[h, :, HEAD:] = unit
            if not diff:
                va_ref[h, :, :HEAD] = v_ref[:, h * HEAD:(h + 1) * HEAD]
                va_ref[h, :, HEAD:] = unit
            k2 = k.astype(F32)
            k2 = k2 * k2
            for g, lanes in enumerate(lane_sets(k2.shape)[:2 if diff else 1]):
                kk = k2 if lanes is None else jnp.where(lanes, k2, 0.0)
                kmax_ref[2 * h + g] = jnp.max(jnp.sum(kk, axis=-1, keepdims=True), axis=0, keepdims=True)

    lane0 = lax.broadcasted_iota(jnp.int32, (bq, HEAD), 1) == 0
    bound = None
    for h in range(heads):
        for g, lanes in enumerate(lane_sets((bq, HEAD))):
            if diff:
                qv = q_ref[:, h * qw:(h + 1) * qw]
                qg = jnp.where(lanes, qv, jnp.zeros_like(qv))
            else:
                qg = q_ref[:, h * qw + g * HEAD:h * qw + (g + 1) * HEAD]
            qf = qg.astype(F32)
            b = jnp.sqrt(jnp.sum(qf * qf, axis=-1, keepdims=True) * kmax_ref[2 * h + (g if diff else 0)])
            qa_ref[h, g * bq:(g + 1) * bq, :HEAD] = qg
            qa_ref[h, g * bq:(g + 1) * bq, HEAD:] = jnp.where(lane0, -b, 0.0).astype(BF16)
            bound = b if bound is None else jnp.maximum(bound, b)
    fast = jnp.max(bound) <= FAST_BOUND

    if diff:
        lam = (jnp.exp(jnp.sum(par_ref[0:1, :] * par_ref[1:2, :], axis=-1, keepdims=True))
               - jnp.exp(jnp.sum(par_ref[2:3, :] * par_ref[3:4, :], axis=-1, keepdims=True))
               + par_ref[5:6, 0:1])

    def finish_diff(h, o):
        o = o * lax.rsqrt(jnp.mean(o * o, axis=-1, keepdims=True) + EPS) * par_ref[4:5, :]
        o_ref[:, h * qw:(h + 1) * qw] = (o * (1.0 - par_ref[5:6, 0:1])).astype(BF16)

    def finalize(get):
        for h in range(heads):
            if diff:
                (a0, l0), (a1, l1) = get(h, 0), get(h, 1)
                finish_diff(h, a0 / l0 - lam * (a1 / l1))
            else:
                for g in range(groups):
                    a, l = get(h, g)
                    o_ref[:, h * qw + g * HEAD:h * qw + (g + 1) * HEAD] = (a / l).astype(BF16)

    def sweep(chunk, kr, vr):
        def body(c, carry):
            sl = pl.ds(pl.multiple_of(c * bk, bk), bk)
            chunk(lambda h: kr(h, sl), lambda h: vr(h, sl))
            return carry

        lax.fori_loop(0, n_main, body, 0)

        @pl.when(is_lat)
        def _():
            sl = slice(last0, n_lat + n_ctx)
            chunk(lambda h: kr(h, sl), lambda h: vr(h, sl))

        @pl.when(jnp.logical_not(is_lat))
        def _():
            sl = slice(n_lat, n_lat + n_ctx)
            chunk(lambda h: kr(h, sl), lambda h: vr(h, sl))

    def fast_gqa():
        acc_ref[...] = jnp.zeros(acc_ref.shape, F32)

        def chunk(k, v):
            for h in range(heads):
                p = jnp.exp2(_dot_nt(qa_ref[h], k(h))).astype(BF16)
                acc_ref[h] += _dot(p, v(h))

        sweep(chunk, lambda h, sl: ka_ref[h, sl, :], lambda h, sl: va_ref[h, sl, :])
        finalize(lambda h, g: (acc_ref[h, g * bq:(g + 1) * bq, :HEAD],
                               acc_ref[h, g * bq:(g + 1) * bq, HEAD:HEAD + 1]))

    def fast_diff():
        lat = [slice(c * bk, (c + 1) * bk) for c in range(n_lat // bk)]
        ctx = slice(n_lat, n_lat + n_ctx)
        out = acc_ref.at[0, 0:bq, 0:HEAD]
        lsum_ref[...] = jnp.zeros(lsum_ref.shape, F32)

        def pass1(sl):
            p = jnp.exp2(_dot_nt(qa_ref[0], ka_ref[0, sl, :]))
            p_ref[:, sl] = p
            part = p[:, 0:HEAD]
            for t in range(1, (sl.stop - sl.start) // HEAD):
                part = part + p[:, t * HEAD:(t + 1) * HEAD]
            lsum_ref[...] += part

        @pl.when(is_lat)
        def _():
            for sl in lat:
                pass1(sl)

        pass1(ctx)
        l = jnp.sum(lsum_ref[...], axis=-1, keepdims=True)
        r0 = 1.0 / l[0:bq]
        r1 = lam / l[bq:2 * bq]
        out[...] = jnp.zeros((bq, HEAD), F32)

        def pass2(sl):
            a = (p_ref[0:bq, sl] * r0 - p_ref[bq:2 * bq, sl] * r1).astype(BF16)
            out[...] += _dot(a, v_ref[sl, :])

        @pl.when(is_lat)
        def _():
            for sl in lat:
                pass2(sl)

        pass2(ctx)
        finish_diff(0, out[...])

    pl.when(fast)(fast_diff if diff else fast_gqa)

    @pl.when(jnp.logical_not(fast))
    def _():
        m_ref[...] = jnp.full(m_ref.shape, NEG, F32)
        l_ref[...] = jnp.zeros(l_ref.shape, F32)
        acc2_ref[...] = jnp.zeros(acc2_ref.shape, F32)

        def chunk(k, v):
            for h in range(heads):
                for g in range(groups):
                    r = h * groups + g
                    s = _dot_nt(qa_ref[h, g * bq:(g + 1) * bq, :HEAD], k(h))
                    m_prev = m_ref[r]
                    m_new = jnp.maximum(m_prev, jnp.max(s, axis=-1, keepdims=True))
                    alpha = jnp.exp2(m_prev - m_new)
                    p = jnp.exp2(s - m_new)
                    l_ref[r] = alpha * l_ref[r] + jnp.sum(p, axis=-1, keepdims=True)
                    acc2_ref[r] = alpha * acc2_ref[r] + _dot(p.astype(BF16), v(h))
                    m_ref[r] = m_new

        sweep(chunk, lambda h, sl: k_ref[sl, h * HEAD:(h + 1) * HEAD],
              lambda h, sl: v_ref[sl, h * HEAD:(h + 1) * HEAD])
        finalize(lambda h, g: (acc2_ref[h * groups + g], l_ref[h * groups + g]))


def _flash(qkv, par, layer, *, n_kv_heads, heads, groups, diff, bq, bk, q0, k0, v0, n_lat, n_ctx):
    ntok = qkv.shape[0]
    qw = heads * (HEAD if diff else groups * HEAD)
    kw = heads * HEAD
    assert n_lat % bq == 0 and n_ctx % bq == 0 and n_lat % bk == 0 and (heads == 1 or not diff)
    return pl.pallas_call(
        functools.partial(_flash_kernel, heads=heads, groups=groups, diff=diff, bk=bk,
                          n_lat=n_lat, n_ctx=n_ctx),
        out_shape=jax.ShapeDtypeStruct((ntok, n_kv_heads // heads * qw), BF16),
        grid=(n_kv_heads // heads, ntok // bq),
        in_specs=[
            pl.BlockSpec((bq, qw), lambda h, i: (i, q0 * HEAD // qw + h)),
            pl.BlockSpec((ntok, kw), lambda h, i: (0, k0 * HEAD // kw + h)),
            pl.BlockSpec((ntok, kw), lambda h, i: (0, v0 * HEAD // kw + h)),
            _layer_spec(par.shape[1:], lambda h, i: (0, 0), layer),
        ],
        out_specs=pl.BlockSpec((bq, qw), lambda h, i: (i, h)),
        scratch_shapes=[
            pltpu.VMEM((heads, ntok, MXU_DIM), BF16),
            pltpu.VMEM((heads, HALO if diff else ntok, MXU_DIM), BF16),
            pltpu.VMEM((2 * heads, 1, 1), F32),
            pltpu.VMEM((heads, groups * bq, MXU_DIM), BF16),
            pltpu.VMEM((heads, groups * bq, MXU_DIM), F32),
            pltpu.VMEM((heads * groups, bq, 1), F32),
            pltpu.VMEM((heads * groups, bq, 1), F32),
            pltpu.VMEM((heads * groups, bq, HEAD), F32),
            pltpu.VMEM((groups * bq, ntok) if diff else (8, HEAD), F32),
            pltpu.VMEM((groups * bq, HEAD) if diff else (8, HEAD), F32),
        ],
        compiler_params=_cparams(("arbitrary", "arbitrary")),
        name="flash_diff" if diff else "flash_gqa",
    )(qkv, qkv, qkv, par)


def _rpb_blocks_kernel(rpb_ref, o_ref):
    n_dr, n_dc = 2 * NA_KH - 1, 2 * NA_KW - 1
    base = pl.program_id(0) * (n_dr * n_dc)
    qc = lax.broadcasted_iota(jnp.int32, (GRID_W, GRID_W), 0)
    kc = lax.broadcasted_iota(jnp.int32, (GRID_W, GRID_W), 1)
    cs = jnp.clip(qc - NA_KW // 2, 0, GRID_W - NA_KW)
    in_win = (kc >= cs) & (kc < cs + NA_KW)
    dc = kc - qc + (NA_KW - 1)
    for a in range(n_dr):
        blk = jnp.zeros((GRID_W, GRID_W), F32)
        for b in range(n_dc):
            blk = jnp.where(dc == b, rpb_ref[base + a * n_dc + b], blk)
        o_ref[a] = jnp.where(in_win, blk * LOG2E, NEG)


def _rpb_blocks(rpb):
    lh = rpb.shape[0] * rpb.shape[1]
    n_dr = 2 * NA_KH - 1
    return pl.pallas_call(
        _rpb_blocks_kernel,
        out_shape=jax.ShapeDtypeStruct((lh, n_dr, GRID_W, GRID_W), F32),
        grid=(lh,),
        in_specs=[pl.BlockSpec(memory_space=pltpu.SMEM)],
        out_specs=pl.BlockSpec((None, n_dr, GRID_W, GRID_W), lambda p: (p, 0, 0, 0)),
        compiler_params=_cparams(("arbitrary",)),
        name="rpb_blocks",
    )(rpb.reshape(-1))


def _nbr_window_plan(rows):
    nblk = rows // NBR_QROWS

    def plan(blk):
        r0 = blk * NBR_QROWS
        k0 = min(max(r0 - NA_KH // 2, 0), rows - NBR_KROWS)
        dr = np.zeros((NBR_QROWS, NBR_KROWS), np.int32)
        ok = np.zeros((NBR_QROWS, NBR_KROWS), bool)
        for qr in range(NBR_QROWS):
            r = r0 + qr
            rs = min(max(r - NA_KH // 2, 0), rows - NA_KH)
            for kr in range(NBR_KROWS):
                ok[qr, kr] = rs <= k0 + kr < rs + NA_KH
                dr[qr, kr] = min(max(k0 + kr - r + NA_KH - 1, 0), 2 * NA_KH - 2)
        return dr, ok

    classes = [plan(0), plan(1), plan(nblk - 1)]
    for blk in range(1, nblk - 1):
        dr, ok = plan(blk)
        assert (dr[ok] == classes[1][0][ok]).all() and (ok == classes[1][1]).all()
    return np.stack([c[0] for c in classes]), np.stack([c[1] for c in classes])


def _nbr_bias_tiles(blocks, rows):
    dr, ok = _nbr_window_plan(rows)
    masked = jnp.full((blocks.shape[0], GRID_W, GRID_W), NEG, F32)
    classes = []
    for c in range(dr.shape[0]):
        strips = [jnp.concatenate([blocks[:, dr[c, qr, kr]] if ok[c, qr, kr] else masked
                                   for kr in range(NBR_KROWS)], axis=-1) for qr in range(NBR_QROWS)]
        classes.append(jnp.concatenate(strips, axis=-2))
    return jnp.stack(classes, axis=1)


def _nbr_kernel(q_ref, k_ref, v_ref, bias_ref, o_ref, *, n_lat, n_ctx):
    i = pl.program_id(1)
    bq = q_ref.shape[0]
    rows = n_lat // GRID_W
    head = lambda ref, h, sl: ref[sl, h * HEAD:(h + 1) * HEAD]
    ctx = slice(n_lat, n_lat + n_ctx)

    @pl.when(i * bq < n_lat)
    def _():
        k0 = jnp.clip(i * NBR_QROWS - NA_KH // 2, 0, rows - NBR_KROWS)
        win = pl.ds(pl.multiple_of(k0 * GRID_W, NBR_QROWS * GRID_W), NBR_KROWS * GRID_W)
        for h in range(NBR_HEADS):
            q = head(q_ref, h, slice(None))
            s_ctx = _dot_nt(q, head(k_ref, h, ctx))
            s_win = _dot_nt(q, head(k_ref, h, win)) + bias_ref[h]
            m = jnp.maximum(jnp.max(s_ctx, axis=-1, keepdims=True), jnp.max(s_win, axis=-1, keepdims=True))
            p_win = jnp.exp2(s_win - m)
            p_ctx = jnp.exp2(s_ctx - m)
            l = jnp.sum(p_win, axis=-1, keepdims=True) + jnp.sum(p_ctx, axis=-1, keepdims=True)
            o = _dot(p_win.astype(BF16), head(v_ref, h, win)) + _dot(p_ctx.astype(BF16), head(v_ref, h, ctx))
            o_ref[:, h * HEAD:(h + 1) * HEAD] = (o / l).astype(BF16)

    @pl.when(i * bq >= n_lat)
    def _():
        for h in range(NBR_HEADS):
            s_ctx = _dot_nt(head(q_ref, h, slice(None)), head(k_ref, h, ctx))
            p_ctx = jnp.exp2(s_ctx - jnp.max(s_ctx, axis=-1, keepdims=True))
            l = jnp.sum(p_ctx, axis=-1, keepdims=True)
            o_ref[:, h * HEAD:(h + 1) * HEAD] = (_dot(p_ctx.astype(BF16), head(v_ref, h, ctx)) / l).astype(BF16)


def _nbr_attention(qkv, bias, layer, *, n_lat, n_ctx):
    ntok = qkv.shape[0]
    n_lat_blocks = n_lat // BQ
    nk = NBR_KROWS * GRID_W
    w = NBR_HEADS * HEAD
    groups = B_HEADS // NBR_HEADS

    def bias_map(h, i):
        cls = jnp.where(i == 0, 0, jnp.where(i == n_lat_blocks - 1, 2, 1))
        return (layer * groups + h, cls, 0, 0)

    return pl.pallas_call(
        functools.partial(_nbr_kernel, n_lat=n_lat, n_ctx=n_ctx),
        out_shape=jax.ShapeDtypeStruct((ntok, B_HEADS * HEAD), BF16),
        grid=(groups, ntok // BQ),
        in_specs=[
            pl.BlockSpec((BQ, w), lambda h, i: (i, QB0 * HEAD // w + h)),
            pl.BlockSpec((ntok, w), lambda h, i: (0, KB0 * HEAD // w + h)),
            pl.BlockSpec((ntok, w), lambda h, i: (0, VB0 * HEAD // w + h)),
            pl.BlockSpec((NBR_HEADS, None, BQ, nk), bias_map),
        ],
        out_specs=pl.BlockSpec((BQ, w), lambda h, i: (i, h)),
        compiler_params=_cparams(("arbitrary", "arbitrary")),
        name="nbr_attn",
    )(qkv, qkv, qkv, bias)


def _merge_kernel(u_ref, oa_ref, ob_ref, oc_ref, wg_ref, wa_ref, wb_ref, wc_ref, o_ref):
    u = u_ref[...]
    merged = _sigmoid(_dot(u, wg_ref[0])) * _dot(oa_ref[...], wa_ref[...])
    merged += _sigmoid(_dot(u, wg_ref[1])) * _dot(ob_ref[...], wb_ref[...])
    merged += _sigmoid(_dot(u, wg_ref[2])) * _dot(oc_ref[...], wc_ref[...])
    o_ref[...] = merged.astype(BF16)


def _merge(u, oa, ob, oc, wg, wa, wb, wc, layer):
    ntok, d = u.shape
    return pl.pallas_call(
        _merge_kernel,
        out_shape=jax.ShapeDtypeStruct((ntok, d), BF16),
        grid=(ntok // BM, d // BN_MERGE),
        in_specs=[
            pl.BlockSpec((BM, d), lambda i, j: (i, 0)),
            pl.BlockSpec((BM, oa.shape[1]), lambda i, j: (i, 0)),
            pl.BlockSpec((BM, ob.shape[1]), lambda i, j: (i, 0)),
            pl.BlockSpec((BM, oc.shape[1]), lambda i, j: (i, 0)),
            _layer_spec((3, d, BN_MERGE), lambda i, j: (0, 0, j), layer),
            _layer_spec((wa.shape[1], BN_MERGE), lambda i, j: (0, j), layer),
            _layer_spec((wb.shape[1], BN_MERGE), lambda i, j: (0, j), layer),
            _layer_spec((wc.shape[1], BN_MERGE), lambda i, j: (0, j), layer),
        ],
        out_specs=pl.BlockSpec((BM, BN_MERGE), lambda i, j: (i, j)),
        compiler_params=_cparams(("arbitrary", "arbitrary")),
        name="branch_merge",
    )(u, oa, ob, oc, wg, wa, wb, wc)


def _proj_res_kernel(a_ref, w_ref, h_ref, gate_ref, o_ref, *, n_lat):
    bm = h_ref.shape[0]
    rows = pl.program_id(0) * bm + lax.broadcasted_iota(jnp.int32, (bm, 1), 0)
    gate = jnp.where(rows >= n_lat, gate_ref[1:2, :], gate_ref[0:1, :])
    o_ref[...] = h_ref[...] + gate * _dot(a_ref[...], w_ref[...])


def _proj_res(a, w, h, mod, gate_col, layer, n_lat, name):
    ntok, d = h.shape
    kdim = a.shape[1]
    return pl.pallas_call(
        functools.partial(_proj_res_kernel, n_lat=n_lat),
        out_shape=jax.ShapeDtypeStruct((ntok, d), F32),
        grid=(ntok // BM, d // BN),
        in_specs=[
            pl.BlockSpec((BM, kdim), lambda i, j: (i, 0)),
            _layer_spec((kdim, BN), lambda i, j: (0, j), layer),
            pl.BlockSpec((BM, BN), lambda i, j: (i, j)),
            _layer_spec((8, BN), lambda i, j: (0, gate_col // BN + j), layer),
        ],
        out_specs=pl.BlockSpec((BM, BN), lambda i, j: (i, j)),
        compiler_params=_cparams(("arbitrary", "arbitrary")),
        name=name,
    )(a, w, h, mod)


def _ffn_up_kernel(h_ref, hp_ref, hn_ref, mod_ref, g_ref, wa_ref, wg_ref, cwa_ref, cwg_ref,
                   cba_ref, cbg_ref, o_ref, u_ref, ya_ref, yg_ref, *, n_lat, n_tok):
    i, j = pl.program_id(0), pl.program_id(1)
    bm = h_ref.shape[0]
    d = h_ref.shape[1]
    col = 3 * d
    row0 = i * bm

    @pl.when(j == 0)
    def _():
        _norm_modulate(hp_ref, u_ref, 0, g_ref, mod_ref, col, row0 - HALO, n_lat)
        _norm_modulate(h_ref, u_ref, HALO, g_ref, mod_ref, col, row0, n_lat)
        _norm_modulate(hn_ref, u_ref, HALO + bm, g_ref, mod_ref, col, row0 + bm, n_lat)

    rows = row0 + lax.broadcasted_iota(jnp.int32, (bm, 1), 0)
    has_prev = ((rows != 0) & (rows != n_lat)).astype(F32)
    has_next = ((rows != n_lat - 1) & (rows != n_tok - 1)).astype(F32)

    def conv(y_ref, cw_ref, cb_ref, cols):
        prev = y_ref[pl.ds(HALO - 1, bm), cols] * has_prev
        cur = y_ref[pl.ds(HALO, bm), cols]
        nxt = y_ref[pl.ds(HALO + 1, bm), cols] * has_next
        return cb_ref[:, cols] + prev * cw_ref[0:1, cols] + cur * cw_ref[1:2, cols] + nxt * cw_ref[2:3, cols]

    u = u_ref[...]
    for s in range(o_ref.shape[1] // FFN_SUB):
        cols = slice(s * FFN_SUB, (s + 1) * FFN_SUB)
        ya_ref[:, cols] = _dot(u, wa_ref[:, cols])
        yg_ref[:, cols] = _dot(u, wg_ref[:, cols])
        a = conv(ya_ref, cwa_ref, cba_ref, cols)
        gt = conv(yg_ref, cwg_ref, cbg_ref, cols)
        o_ref[:, cols] = (gt * _sigmoid(gt) * a).astype(BF16)


def _ffn_up(h, mod, g, wa, wg, cwa, cwg, cba, cbg, layer, n_lat):
    ntok, d = h.shape
    ff = wa.shape[-1]
    per = BM // HALO
    last = ntok // HALO - 1
    return pl.pallas_call(
        functools.partial(_ffn_up_kernel, n_lat=n_lat, n_tok=ntok),
        out_shape=jax.ShapeDtypeStruct((ntok, ff), BF16),
        grid=(ntok // BM, ff // BN),
        in_specs=[
            pl.BlockSpec((BM, d), lambda i, j: (i, 0)),
            pl.BlockSpec((HALO, d), lambda i, j: (jnp.maximum(i * per - 1, 0), 0)),
            pl.BlockSpec((HALO, d), lambda i, j: (jnp.minimum((i + 1) * per, last), 0)),
            _layer_spec(mod.shape[1:], lambda i, j: (0, 0), layer),
            _layer_spec((1, d), lambda i, j: (0, 0), layer),
            _layer_spec((d, BN), lambda i, j: (0, j), layer),
            _layer_spec((d, BN), lambda i, j: (0, j), layer),
            _layer_spec((3, BN), lambda i, j: (0, j), layer),
            _layer_spec((3, BN), lambda i, j: (0, j), layer),
            _layer_spec((1, BN), lambda i, j: (0, j), layer),
            _layer_spec((1, BN), lambda i, j: (0, j), layer),
        ],
        out_specs=pl.BlockSpec((BM, BN), lambda i, j: (i, j)),
        scratch_shapes=[
            pltpu.VMEM((BM + 2 * HALO, d), BF16),
            pltpu.VMEM((BM + 2 * HALO, BN), F32),
            pltpu.VMEM((BM + 2 * HALO, BN), F32),
        ],
        compiler_params=_cparams(("arbitrary", "arbitrary")),
        name="ffn_up",
    )(h, h, h, mod, g, wa, wg, cwa, cwg, cba, cbg)


def _rot_layout_a(v):
    sh = v.shape
    return jnp.swapaxes(v.reshape(sh[:-1] + (-1, 2, 2, HEAD // 4)), -3, -2).reshape(sh)


def _rot_layout_c(v):
    sh = v.shape
    return jnp.moveaxis(v.reshape(sh[:-1] + (-1, 2, 2, 2, C_HALF // 4)), -2, -4).reshape(sh)


def _rope_tables(n_lat, n_ctx, dim):
    n_freq = dim // 4
    inv_freq = ROPE_THETA ** (-jnp.arange(n_freq, dtype=F32) / n_freq)
    t = jnp.arange(n_lat, dtype=jnp.int32)
    pos = jnp.stack([t // GRID_W, t % GRID_W], axis=-1).astype(F32)
    ang = (pos[:, :, None] * inv_freq).reshape(n_lat, 2 * n_freq)
    reps = HALF // (2 * n_freq)
    cos_h = jnp.tile(jnp.cos(ang), (1, reps))
    sin_h = jnp.tile(jnp.sin(ang), (1, reps))
    cos_l = jnp.concatenate([cos_h, cos_h], axis=-1)
    sin_l = jnp.concatenate([-sin_h, sin_h], axis=-1)
    cos_l = jnp.concatenate([cos_l, jnp.ones((n_ctx, HEAD), F32)], axis=0)
    sin_l = jnp.concatenate([sin_l, jnp.zeros((n_ctx, HEAD), F32)], axis=0)
    return cos_l, sin_l


def kernel(x, c, ctx, c_ctx, w_mod, b_mod, g_norm1, w_in, gq_a, gk_a, gq_b, gk_b, rpb_b, gq_c, gk_c, lam_q1, lam_k1, lam_q2, lam_k2, g_subln_c, w_br_a, w_br_b, w_br_c, w_o, g_norm2, w_up, conv_w, conv_b, w_down):
    batch, n_lat, d = x.shape
    n_ctx = ctx.shape[1]
    depth = w_mod.shape[0]
    d_ff = w_down.shape[1]
    assert batch == 1 and d == 16 * HEAD
    assert (n_lat + n_ctx) % BM == 0 and n_lat % BQ == 0 and n_ctx == BQ
    ff_pad = -(-d_ff // BN) * BN

    col = lambda a, b: w_in[:, :, a * HEAD:b * HEAD].astype(BF16)
    wqkv = jnp.concatenate([
        _rot_layout_a(col(QA0, KA0)), _rot_layout_a(col(KA0, VA0)), col(VA0, QC0),
        _rot_layout_c(col(QC0, KC0)), _rot_layout_c(col(KC0, VC0)), col(VC0, VC0 + C_HEADS)], axis=-1)
    wgate = jnp.transpose(w_in[:, :, N_QKV:].reshape(depth, d, 3, d), (0, 2, 1, 3)).astype(BF16)
    wbra, wbrb, wbrc, wo = (w.astype(BF16) for w in (w_br_a, w_br_b, w_br_c, w_o))
    padc = ((0, 0), (0, 0), (0, ff_pad - d_ff))
    zcols = jnp.zeros((depth, d, ff_pad - d_ff), BF16)
    wupa = jnp.concatenate([w_up[:, :, :d_ff].astype(BF16), zcols], axis=-1)
    wupg = jnp.concatenate([w_up[:, :, d_ff:].astype(BF16), zcols], axis=-1)
    cwa = jnp.pad(conv_w[:, :, :d_ff], padc)
    cwg = jnp.pad(conv_w[:, :, d_ff:], padc)
    cba = jnp.pad(conv_b[:, None, :d_ff], padc)
    cbg = jnp.pad(conv_b[:, None, d_ff:], padc)
    wdown = jnp.concatenate([w_down.astype(BF16), jnp.zeros((depth, ff_pad - d_ff, d), BF16)], axis=1)

    ones = jnp.ones((depth, HEAD), F32)
    gains = jnp.stack([_rot_layout_a(gq_a), _rot_layout_a(gk_a), gq_b, gk_b,
                       _rot_layout_c(jnp.tile(gq_c, (1, 2))), _rot_layout_c(jnp.tile(gk_c, (1, 2))),
                       ones, ones], axis=1)
    pad64 = lambda v: jnp.pad(v, ((0, 0), (0, HEAD - C_HALF)))
    lam_init = jnp.asarray([0.8 - 0.6 * math.exp(-0.3 * l) for l in range(depth)], F32)
    cpar = jnp.stack([pad64(lam_q1), pad64(lam_k1), pad64(lam_q2), pad64(lam_k2), g_subln_c,
                      jnp.broadcast_to(lam_init[:, None], (depth, HEAD)), ones, ones], axis=1)
    g1 = g_norm1[:, None, :]
    g2 = g_norm2[:, None, :]

    rope = _rope_tables(n_lat, n_ctx, HEAD) + _rope_tables(n_lat, n_ctx, C_HALF)
    cc = jnp.zeros((8, d), F32).at[0].set(c[0]).at[1].set(c_ctx)

    mod = _modulation(cc, w_mod, b_mod)
    bias = _nbr_bias_tiles(_rpb_blocks(rpb_b), n_lat // GRID_W)

    h = jnp.concatenate([x[0], ctx[0]], axis=0)
    att = dict(n_lat=n_lat, n_ctx=n_ctx)
    for l in range(depth):
        qkv, u = _qkv_proj(h, mod, g1, wqkv, gains, rope, l, n_lat)
        oa = _flash(qkv, cpar, l, n_kv_heads=A_KV, heads=1, groups=A_HEADS // A_KV, diff=False,
                    bq=BQ, bk=min(BK_GQA, n_lat), q0=QA0, k0=KA0, v0=VA0, **att)
        ob = _nbr_attention(qkv, bias, l, **att)
        oc = _flash(qkv, cpar, l, n_kv_heads=C_HEADS, heads=1, groups=2, diff=True,
                    bq=BQ, bk=min(BK_DIFF, n_lat), q0=QC0, k0=KC0, v0=VC0, **att)
        merged = _merge(u, oa, ob, oc, wgate, wbra, wbrb, wbrc, l)
        h = _proj_res(merged, wo, h, mod, 2 * d, l, n_lat, "out_proj")
        act = _ffn_up(h, mod, g2, wupa, wupg, cwa, cwg, cba, cbg, l, n_lat)
        h = _proj_res(act, wdown, h, mod, 5 * d, l, n_lat, "ffn_down")
    return h[:n_lat][None]
```

```python
import functools
import math

import numpy as np
import jax
import jax.numpy as jnp
from jax import lax
from jax.experimental import pallas as pl
from jax.experimental.pallas import tpu as pltpu

F32 = jnp.float32
BF16 = jnp.bfloat16

HEAD = 128
HALF = HEAD // 2
GRID_W = 64
A_HEADS, A_KV = 8, 2
B_HEADS = 4
C_HEADS = 4
C_HALF = 64
NA_KH, NA_KW = 8, 16
ROPE_THETA = 10000.0
EPS = 1e-6
NEG = -1e30
LOG2E = math.log2(math.e)
Q_SCALE_AB = HEAD ** -0.5 * LOG2E
Q_SCALE_C = C_HALF ** -0.5 * LOG2E

QA0, KA0, VA0 = 0, 8, 10
QB0, KB0, VB0 = 12, 16, 20
QC0, KC0, VC0 = 24, 28, 32
N_QKV = 36 * HEAD

VMEM_LIMIT = 56 * 1024 * 1024
MXU_DIM = 256
BM = 768
BN = 512
BN_MERGE = 512
FFN_SUB = 256
HALO = 16
NORM_ROWS = 16
BQ = 256
BK_GQA = 4096
BK_DIFF = 2048
NBR_HEADS = 2
NBR_QROWS = BQ // GRID_W
NBR_KROWS = NBR_QROWS + NA_KH
FAST_BOUND = 48.0


def _cparams(sem):
    return pltpu.CompilerParams(dimension_semantics=sem, vmem_limit_bytes=VMEM_LIMIT)


def _sigmoid(x):
    return 1.0 / (1.0 + jnp.exp(-x))


def _dot(a, b):
    return jnp.dot(a, b, preferred_element_type=F32)


def _dot_nt(a, b):
    return lax.dot_general(a, b, (((1,), (1,)), ((), ())), preferred_element_type=F32)


def _layer_spec(block, index_map, layer):
    return pl.BlockSpec((None,) + tuple(block), lambda *g: (layer,) + tuple(index_map(*g)))


def _mod_kernel(cc_ref, w_ref, b_ref, o_ref):
    cc = cc_ref[...]
    a = (cc * _sigmoid(cc)).astype(BF16)
    o_ref[...] = _dot(a, w_ref[...].astype(BF16)) + b_ref[...]


def _modulation(cc, w_mod, b_mod):
    depth, d, n = w_mod.shape
    bn = 1024
    return pl.pallas_call(
        _mod_kernel,
        out_shape=jax.ShapeDtypeStruct((depth, 8, n), F32),
        grid=(depth, n // bn),
        in_specs=[
            pl.BlockSpec((8, d), lambda l, j: (0, 0)),
            pl.BlockSpec((None, d, bn), lambda l, j: (l, 0, j)),
            pl.BlockSpec((None, 1, bn), lambda l, j: (l, 0, j)),
        ],
        out_specs=pl.BlockSpec((None, 8, bn), lambda l, j: (l, 0, j)),
        compiler_params=_cparams(("arbitrary", "arbitrary")),
        name="adaln_mod",
    )(cc, w_mod, b_mod.reshape(depth, 1, n))


def _norm_modulate(h_ref, u_ref, u_row0, g_ref, mod_ref, col0, row0, n_lat):
    n, d = h_ref.shape
    g = g_ref[...]

    def body(c, carry):
        r = pl.multiple_of(c * NORM_ROWS, NORM_ROWS)
        x = h_ref[pl.ds(r, NORM_ROWS), :]
        y = x * lax.rsqrt(jnp.mean(x * x, axis=-1, keepdims=True) + EPS) * g
        who = (row0 + r >= n_lat).astype(jnp.int32)
        shift = mod_ref[pl.ds(who, 1), col0:col0 + d]
        scale = mod_ref[pl.ds(who, 1), col0 + d:col0 + 2 * d]
        dst = pl.multiple_of(u_row0 + r, NORM_ROWS)
        u_ref[pl.ds(dst, NORM_ROWS), :] = (y * (1.0 + scale) + shift).astype(BF16)
        return carry

    trips = n // NORM_ROWS
    lax.fori_loop(0, trips, body, 0, unroll=math.gcd(trips, 4))


def _diff_lanes(shape):
    return (lax.broadcasted_iota(jnp.int32, shape, 1) % HALF) < HALF // 2


def _rms_heads(x, gain, two_components):
    x2 = x * x
    if not two_components:
        ms = jnp.mean(x2, axis=-1, keepdims=True)
    else:
        first = _diff_lanes(x.shape)
        s0 = jnp.sum(jnp.where(first, x2, 0.0), axis=-1, keepdims=True)
        s1 = jnp.sum(jnp.where(first, 0.0, x2), axis=-1, keepdims=True)
        ms = jnp.where(first, s0, s1) * (1.0 / C_HALF)
    return x * lax.rsqrt(ms + EPS) * gain


def _rope(x, cos, sin_signed):
    return x * cos + pltpu.roll(x, HALF, 1) * sin_signed


_RAW = ("raw", 0, 1.0)
_QKV_BLOCKS = (
    (("rope_a", 0, Q_SCALE_AB),) * 4, (("rope_a", 0, Q_SCALE_AB),) * 4,
    (("rope_a", 1, 1.0), ("rope_a", 1, 1.0), _RAW, _RAW),
    (("norm", 2, Q_SCALE_AB),) * 4, (("norm", 3, 1.0),) * 4, (_RAW,) * 4,
    (("rope_c", 4, Q_SCALE_C),) * 4, (("rope_c", 5, 1.0),) * 4, (_RAW,) * 4,
)


def _qkv_kernel(h_ref, mod_ref, g_ref, w_ref, gains_ref, ca_ref, sa_ref, cc_ref, sc_ref,
                o_ref, u_ref, *, n_lat):
    i, j = pl.program_id(0), pl.program_id(1)
    bm = h_ref.shape[0]

    @pl.when(j == 0)
    def _():
        _norm_modulate(h_ref, u_ref, 0, g_ref, mod_ref, 0, i * bm, n_lat)

    acc = _dot(u_ref[...], w_ref[...])

    def epilogue(heads):
        for k, (kind, gr, out_scale) in enumerate(heads):
            x = acc[:, k * HEAD:(k + 1) * HEAD]
            gain = gains_ref[gr:gr + 1, :] * out_scale
            if kind == "rope_a":
                x = _rope(_rms_heads(x, gain, False), ca_ref[...], sa_ref[...])
            elif kind == "norm":
                x = _rms_heads(x, gain, False)
            elif kind == "rope_c":
                x = _rope(_rms_heads(x, gain, True), cc_ref[...], sc_ref[...])
            o_ref[:, k * HEAD:(k + 1) * HEAD] = x.astype(BF16)

    for heads in sorted(set(_QKV_BLOCKS)):
        cond = None
        for jj, hh in enumerate(_QKV_BLOCKS):
            if hh == heads:
                cond = (j == jj) if cond is None else (cond | (j == jj))
        pl.when(cond)(functools.partial(epilogue, heads))


def _qkv_proj(h, mod, g, w, gains, rope, layer, n_lat):
    ntok, d = h.shape
    n = w.shape[-1]
    ca, sa, cc, sc = rope
    return pl.pallas_call(
        functools.partial(_qkv_kernel, n_lat=n_lat),
        out_shape=(jax.ShapeDtypeStruct((ntok, n), BF16), jax.ShapeDtypeStruct((ntok, d), BF16)),
        grid=(ntok // BM, n // BN),
        in_specs=[
            pl.BlockSpec((BM, d), lambda i, j: (i, 0)),
            _layer_spec(mod.shape[1:], lambda i, j: (0, 0), layer),
            _layer_spec((1, d), lambda i, j: (0, 0), layer),
            _layer_spec((d, BN), lambda i, j: (0, j), layer),
            _layer_spec(gains.shape[1:], lambda i, j: (0, 0), layer),
            pl.BlockSpec((BM, HEAD), lambda i, j: (i, 0)),
            pl.BlockSpec((BM, HEAD), lambda i, j: (i, 0)),
            pl.BlockSpec((BM, HEAD), lambda i, j: (i, 0)),
            pl.BlockSpec((BM, HEAD), lambda i, j: (i, 0)),
        ],
        out_specs=(pl.BlockSpec((BM, BN), lambda i, j: (i, j)),
                   pl.BlockSpec((BM, d), lambda i, j: (i, 0))),
        compiler_params=_cparams(("arbitrary", "arbitrary")),
        name="qkv_proj",
    )(h, mod, g, w, gains, ca, sa, cc, sc)


def _flash_kernel(q_ref, k_ref, v_ref, par_ref, o_ref,
                  ka_ref, va_ref, kmax_ref, qa_ref, acc_ref, m_ref, l_ref, acc2_ref, p_ref, lsum_ref,
                  *, heads, groups, diff, bk, n_lat, n_ctx):
    i = pl.program_id(1)
    bq = q_ref.shape[0]
    ntok = k_ref.shape[0]
    qw = q_ref.shape[1] // heads
    is_lat = i * bq < n_lat
    n_main = jnp.where(is_lat, n_lat // bk - 1, 0)
    last0 = n_lat - bk

    def lane_sets(shape):
        if not diff:
            return [None] * groups
        first = _diff_lanes(shape)
        return [first, jnp.logical_not(first)]

    @pl.when(i == 0)
    def _():
        unit = (lax.broadcasted_iota(jnp.int32, (ntok, HEAD), 1) == 0).astype(F32).astype(BF16)
        for h in range(heads):
            k = k_ref[:, h * HEAD:(h + 1) * HEAD]
            ka_ref[h, :, :HEAD] = k
            ka_ref[h, :, HEAD:] = unit
            if not diff:
                va_ref[h, :, :HEAD] = v_ref[:, h * HEAD:(h + 1) * HEAD]
                va_ref[h, :, HEAD:] = unit
            k2 = k.astype(F32)
            k2 = k2 * k2
            for g, lanes in enumerate(lane_sets(k2.shape)[:2 if diff else 1]):
                kk = k2 if lanes is None else jnp.where(lanes, k2, 0.0)
                kmax_ref[2 * h + g] = jnp.max(jnp.sum(kk, axis=-1, keepdims=True), axis=0, keepdims=True)

    lane0 = lax.broadcasted_iota(jnp.int32, (bq, HEAD), 1) == 0
    bound = None
    for h in range(heads):
        for g, lanes in enumerate(lane_sets((bq, HEAD))):
            if diff:
                qv = q_ref[:, h * qw:(h + 1) * qw]
                qg = jnp.where(lanes, qv, jnp.zeros_like(qv))
            else:
                qg = q_ref[:, h * qw + g * HEAD:h * qw + (g + 1) * HEAD]
            qf = qg.astype(F32)
            b = jnp.sqrt(jnp.sum(qf * qf, axis=-1, keepdims=True) * kmax_ref[2 * h + (g if diff else 0)])
            qa_ref[h, g * bq:(g + 1) * bq, :HEAD] = qg
            qa_ref[h, g * bq:(g + 1) * bq, HEAD:] = jnp.where(lane0, -b, 0.0).astype(BF16)
            bound = b if bound is None else jnp.maximum(bound, b)
    fast = jnp.max(bound) <= FAST_BOUND

    if diff:
        lam = (jnp.exp(jnp.sum(par_ref[0:1, :] * par_ref[1:2, :], axis=-1, keepdims=True))
               - jnp.exp(jnp.sum(par_ref[2:3, :] * par_ref[3:4, :], axis=-1, keepdims=True))
               + par_ref[5:6, 0:1])

    def finish_diff(h, o):
        o = o * lax.rsqrt(jnp.mean(o * o, axis=-1, keepdims=True) + EPS) * par_ref[4:5, :]
        o_ref[:, h * qw:(h + 1) * qw] = (o * (1.0 - par_ref[5:6, 0:1])).astype(BF16)

    def finalize(get):
        for h in range(heads):
            if diff:
                (a0, l0), (a1, l1) = get(h, 0), get(h, 1)
                finish_diff(h, a0 / l0 - lam * (a1 / l1))
            else:
                for g in range(groups):
                    a, l = get(h, g)
                    o_ref[:, h * qw + g * HEAD:h * qw + (g + 1) * HEAD] = (a / l).astype(BF16)

    def sweep(chunk, kr, vr):
        def body(c, carry):
            sl = pl.ds(pl.multiple_of(c * bk, bk), bk)
            chunk(lambda h: kr(h, sl), lambda h: vr(h, sl))
            return carry

        lax.fori_loop(0, n_main, body, 0)

        @pl.when(is_lat)
        def _():
            sl = slice(last0, n_lat + n_ctx)
            chunk(lambda h: kr(h, sl), lambda h: vr(h, sl))

        @pl.when(jnp.logical_not(is_lat))
        def _():
            sl = slice(n_lat, n_lat + n_ctx)
            chunk(lambda h: kr(h, sl), lambda h: vr(h, sl))

    def fast_gqa():
        acc_ref[...] = jnp.zeros(acc_ref.shape, F32)

        def chunk(k, v):
            for h in range(heads):
                p = jnp.exp2(_dot_nt(qa_ref[h], k(h))).astype(BF16)
                acc_ref[h] += _dot(p, v(h))

        sweep(chunk, lambda h, sl: ka_ref[h, sl, :], lambda h, sl: va_ref[h, sl, :])
        finalize(lambda h, g: (acc_ref[h, g * bq:(g + 1) * bq, :HEAD],
                               acc_ref[h, g * bq:(g + 1) * bq, HEAD:HEAD + 1]))

    def fast_diff():
        lat = [slice(c * bk, (c + 1) * bk) for c in range(n_lat // bk)]
        ctx = slice(n_lat, n_lat + n_ctx)
        out = acc_ref.at[0, 0:bq, 0:HEAD]
        lsum_ref[...] = jnp.zeros(lsum_ref.shape, F32)

        def pass1(sl):
            p = jnp.exp2(_dot_nt(qa_ref[0], ka_ref[0, sl, :]))
            p_ref[:, sl] = p
            part = p[:, 0:HEAD]
            for t in range(1, (sl.stop - sl.start) // HEAD):
                part = part + p[:, t * HEAD:(t + 1) * HEAD]
            lsum_ref[...] += part

        @pl.when(is_lat)
        def _():
            for sl in lat:
                pass1(sl)

        pass1(ctx)
        l = jnp.sum(lsum_ref[...], axis=-1, keepdims=True)
        r0 = 1.0 / l[0:bq]
        r1 = lam / l[bq:2 * bq]
        out[...] = jnp.zeros((bq, HEAD), F32)

        def pass2(sl):
            a = (p_ref[0:bq, sl] * r0 - p_ref[bq:2 * bq, sl] * r1).astype(BF16)
            out[...] += _dot(a, v_ref[sl, :])

        @pl.when(is_lat)
        def _():
            for sl in lat:
                pass2(sl)

        pass2(ctx)
        finish_diff(0, out[...])

    pl.when(fast)(fast_diff if diff else fast_gqa)

    @pl.when(jnp.logical_not(fast))
    def _():
        m_ref[...] = jnp.full(m_ref.shape, NEG, F32)
        l_ref[...] = jnp.zeros(l_ref.shape, F32)
        acc2_ref[...] = jnp.zeros(acc2_ref.shape, F32)

        def chunk(k, v):
            for h in range(heads):
                for g in range(groups):
                    r = h * groups + g
                    s = _dot_nt(qa_ref[h, g * bq:(g + 1) * bq, :HEAD], k(h))
                    m_prev = m_ref[r]
                    m_new = jnp.maximum(m_prev, jnp.max(s, axis=-1, keepdims=True))
                    alpha = jnp.exp2(m_prev - m_new)
                    p = jnp.exp2(s - m_new)
                    l_ref[r] = alpha * l_ref[r] + jnp.sum(p, axis=-1, keepdims=True)
                    acc2_ref[r] = alpha * acc2_ref[r] + _dot(p.astype(BF16), v(h))
                    m_ref[r] = m_new

        sweep(chunk, lambda h, sl: k_ref[sl, h * HEAD:(h + 1) * HEAD],
              lambda h, sl: v_ref[sl, h * HEAD:(h + 1) * HEAD])
        finalize(lambda h, g: (acc2_ref[h * groups + g], l_ref[h * groups + g]))


def _flash(qkv, par, layer, *, n_kv_heads, heads, groups, diff, bq, bk, q0, k0, v0, n_lat, n_ctx):
    ntok = qkv.shape[0]
    qw = heads * (HEAD if diff else groups * HEAD)
    kw = heads * HEAD
    assert n_lat % bq == 0 and n_ctx % bq == 0 and n_lat % bk == 0 and (heads == 1 or not diff)
    return pl.pallas_call(
        functools.partial(_flash_kernel, heads=heads, groups=groups, diff=diff, bk=bk,
                          n_lat=n_lat, n_ctx=n_ctx),
        out_shape=jax.ShapeDtypeStruct((ntok, n_kv_heads // heads * qw), BF16),
        grid=(n_kv_heads // heads, ntok // bq),
        in_specs=[
            pl.BlockSpec((bq, qw), lambda h, i: (i, q0 * HEAD // qw + h)),
            pl.BlockSpec((ntok, kw), lambda h, i: (0, k0 * HEAD // kw + h)),
            pl.BlockSpec((ntok, kw), lambda h, i: (0, v0 * HEAD // kw + h)),
            _layer_spec(par.shape[1:], lambda h, i: (0, 0), layer),
        ],
        out_specs=pl.BlockSpec((bq, qw), lambda h, i: (i, h)),
        scratch_shapes=[
            pltpu.VMEM((heads, ntok, MXU_DIM), BF16),
            pltpu.VMEM((heads, HALO if diff else ntok, MXU_DIM), BF16),
            pltpu.VMEM((2 * heads, 1, 1), F32),
            pltpu.VMEM((heads, groups * bq, MXU_DIM), BF16),
            pltpu.VMEM((heads, groups * bq, MXU_DIM), F32),
            pltpu.VMEM((heads * groups, bq, 1), F32),
            pltpu.VMEM((heads * groups, bq, 1), F32),
            pltpu.VMEM((heads * groups, bq, HEAD), F32),
            pltpu.VMEM((groups * bq, ntok) if diff else (8, HEAD), F32),
            pltpu.VMEM((groups * bq, HEAD) if diff else (8, HEAD), F32),
        ],
        compiler_params=_cparams(("arbitrary", "arbitrary")),
        name="flash_diff" if diff else "flash_gqa",
    )(qkv, qkv, qkv, par)


def _rpb_blocks_kernel(rpb_ref, o_ref):
    n_dr, n_dc = 2 * NA_KH - 1, 2 * NA_KW - 1
    base = pl.program_id(0) * (n_dr * n_dc)
    qc = lax.broadcasted_iota(jnp.int32, (GRID_W, GRID_W), 0)
    kc = lax.broadcasted_iota(jnp.int32, (GRID_W, GRID_W), 1)
    cs = jnp.clip(qc - NA_KW // 2, 0, GRID_W - NA_KW)
    in_win = (kc >= cs) & (kc < cs + NA_KW)
    dc = kc - qc + (NA_KW - 1)
    for a in range(n_dr):
        blk = jnp.zeros((GRID_W, GRID_W), F32)
        for b in range(n_dc):
            blk = jnp.where(dc == b, rpb_ref[base + a * n_dc + b], blk)
        o_ref[a] = jnp.where(in_win, blk * LOG2E, NEG)


def _rpb_blocks(rpb):
    lh = rpb.shape[0] * rpb.shape[1]
    n_dr = 2 * NA_KH - 1
    return pl.pallas_call(
        _rpb_blocks_kernel,
        out_shape=jax.ShapeDtypeStruct((lh, n_dr, GRID_W, GRID_W), F32),
        grid=(lh,),
        in_specs=[pl.BlockSpec(memory_space=pltpu.SMEM)],
        out_specs=pl.BlockSpec((None, n_dr, GRID_W, GRID_W), lambda p: (p, 0, 0, 0)),
        compiler_params=_cparams(("arbitrary",)),
        name="rpb_blocks",
    )(rpb.reshape(-1))


def _nbr_window_plan(rows):
    nblk = rows // NBR_QROWS

    def plan(blk):
        r0 = blk * NBR_QROWS
        k0 = min(max(r0 - NA_KH // 2, 0), rows - NBR_KROWS)
        dr = np.zeros((NBR_QROWS, NBR_KROWS), np.int32)
        ok = np.zeros((NBR_QROWS, NBR_KROWS), bool)
        for qr in range(NBR_QROWS):
            r = r0 + qr
            rs = min(max(r - NA_KH // 2, 0), rows - NA_KH)
            for kr in range(NBR_KROWS):
                ok[qr, kr] = rs <= k0 + kr < rs + NA_KH
                dr[qr, kr] = min(max(k0 + kr - r + NA_KH - 1, 0), 2 * NA_KH - 2)
        return dr, ok

    classes = [plan(0), plan(1), plan(nblk - 1)]
    for blk in range(1, nblk - 1):
        dr, ok = plan(blk)
        assert (dr[ok] == classes[1][0][ok]).all() and (ok == classes[1][1]).all()
    return np.stack([c[0] for c in classes]), np.stack([c[1] for c in classes])


def _nbr_bias_tiles(blocks, rows):
    dr, ok = _nbr_window_plan(rows)
    masked = jnp.full((blocks.shape[0], GRID_W, GRID_W), NEG, F32)
    classes = []
    for c in range(dr.shape[0]):
        strips = [jnp.concatenate([blocks[:, dr[c, qr, kr]] if ok[c, qr, kr] else masked
                                   for kr in range(NBR_KROWS)], axis=-1) for qr in range(NBR_QROWS)]
        classes.append(jnp.concatenate(strips, axis=-2))
    return jnp.stack(classes, axis=1)


def _nbr_kernel(q_ref, k_ref, v_ref, bias_ref, o_ref, *, n_lat, n_ctx):
    i = pl.program_id(1)
    bq = q_ref.shape[0]
    rows = n_lat // GRID_W
    head = lambda ref, h, sl: ref[sl, h * HEAD:(h + 1) * HEAD]
    ctx = slice(n_lat, n_lat + n_ctx)

    @pl.when(i * bq < n_lat)
    def _():
        k0 = jnp.clip(i * NBR_QROWS - NA_KH // 2, 0, rows - NBR_KROWS)
        win = pl.ds(pl.multiple_of(k0 * GRID_W, NBR_QROWS * GRID_W), NBR_KROWS * GRID_W)
        for h in range(NBR_HEADS):
            q = head(q_ref, h, slice(None))
            s_ctx = _dot_nt(q, head(k_ref, h, ctx))
            s_win = _dot_nt(q, head(k_ref, h, win)) + bias_ref[h]
            m = jnp.maximum(jnp.max(s_ctx, axis=-1, keepdims=True), jnp.max(s_win, axis=-1, keepdims=True))
            p_win = jnp.exp2(s_win - m)
            p_ctx = jnp.exp2(s_ctx - m)
            l = jnp.sum(p_win, axis=-1, keepdims=True) + jnp.sum(p_ctx, axis=-1, keepdims=True)
            o = _dot(p_win.astype(BF16), head(v_ref, h, win)) + _dot(p_ctx.astype(BF16), head(v_ref, h, ctx))
            o_ref[:, h * HEAD:(h + 1) * HEAD] = (o / l).astype(BF16)

    @pl.when(i * bq >= n_lat)
    def _():
        for h in range(NBR_HEADS):
            s_ctx = _dot_nt(head(q_ref, h, slice(None)), head(k_ref, h, ctx))
            p_ctx = jnp.exp2(s_ctx - jnp.max(s_ctx, axis=-1, keepdims=True))
            l = jnp.sum(p_ctx, axis=-1, keepdims=True)
            o_ref[:, h * HEAD:(h + 1) * HEAD] = (_dot(p_ctx.astype(BF16), head(v_ref, h, ctx)) / l).astype(BF16)


def _nbr_attention(qkv, bias, layer, *, n_lat, n_ctx):
    ntok = qkv.shape[0]
    n_lat_blocks = n_lat // BQ
    nk = NBR_KROWS * GRID_W
    w = NBR_HEADS * HEAD
    groups = B_HEADS // NBR_HEADS

    def bias_map(h, i):
        cls = jnp.where(i == 0, 0, jnp.where(i == n_lat_blocks - 1, 2, 1))
        return (layer * groups + h, cls, 0, 0)

    return pl.pallas_call(
        functools.partial(_nbr_kernel, n_lat=n_lat, n_ctx=n_ctx),
        out_shape=jax.ShapeDtypeStruct((ntok, B_HEADS * HEAD), BF16),
        grid=(groups, ntok // BQ),
        in_specs=[
            pl.BlockSpec((BQ, w), lambda h, i: (i, QB0 * HEAD // w + h)),
            pl.BlockSpec((ntok, w), lambda h, i: (0, KB0 * HEAD // w + h)),
            pl.BlockSpec((ntok, w), lambda h, i: (0, VB0 * HEAD // w + h)),
            pl.BlockSpec((NBR_HEADS, None, BQ, nk), bias_map),
        ],
        out_specs=pl.BlockSpec((BQ, w), lambda h, i: (i, h)),
        compiler_params=_cparams(("arbitrary", "arbitrary")),
        name="nbr_attn",
    )(qkv, qkv, qkv, bias)


def _merge_kernel(u_ref, oa_ref, ob_ref, oc_ref, wga_ref, wgb_ref, wgc_ref, wa_ref, wb_ref, wc_ref, o_ref):
    u = u_ref[...]
    merged = _sigmoid(_dot(u, wga_ref[...].astype(BF16))) * _dot(oa_ref[...], wa_ref[...])
    merged += _sigmoid(_dot(u, wgb_ref[...].astype(BF16))) * _dot(ob_ref[...], wb_ref[...])
    merged += _sigmoid(_dot(u, wgc_ref[...].astype(BF16))) * _dot(oc_ref[...], wc_ref[...])
    o_ref[...] = merged.astype(BF16)


def _merge(u, oa, ob, oc, w_in, wa, wb, wc, layer):
    ntok, d = u.shape
    assert N_QKV % BN_MERGE == 0 and d % BN_MERGE == 0
    gate = lambda b: _layer_spec((d, BN_MERGE), lambda i, j: (0, (N_QKV + b * d) // BN_MERGE + j), layer)
    return pl.pallas_call(
        _merge_kernel,
        out_shape=jax.ShapeDtypeStruct((ntok, d), BF16),
        grid=(ntok // BM, d // BN_MERGE),
        in_specs=[
            pl.BlockSpec((BM, d), lambda i, j: (i, 0)),
            pl.BlockSpec((BM, oa.shape[1]), lambda i, j: (i, 0)),
            pl.BlockSpec((BM, ob.shape[1]), lambda i, j: (i, 0)),
            pl.BlockSpec((BM, oc.shape[1]), lambda i, j: (i, 0)),
            gate(0), gate(1), gate(2),
            _layer_spec((wa.shape[1], BN_MERGE), lambda i, j: (0, j), layer),
            _layer_spec((wb.shape[1], BN_MERGE), lambda i, j: (0, j), layer),
            _layer_spec((wc.shape[1], BN_MERGE), lambda i, j: (0, j), layer),
        ],
        out_specs=pl.BlockSpec((BM, BN_MERGE), lambda i, j: (i, j)),
        compiler_params=_cparams(("arbitrary", "arbitrary")),
        name="branch_merge",
    )(u, oa, ob, oc, w_in, w_in, w_in, wa, wb, wc)


def _proj_res_kernel(a_ref, w_ref, h_ref, gate_ref, o_ref, *, n_lat):
    bm = h_ref.shape[0]
    rows = pl.program_id(0) * bm + lax.broadcasted_iota(jnp.int32, (bm, 1), 0)
    gate = jnp.where(rows >= n_lat, gate_ref[1:2, :], gate_ref[0:1, :])
    o_ref[...] = h_ref[...] + gate * _dot(a_ref[...], w_ref[...])


def _proj_res(a, w, h, mod, gate_col, layer, n_lat, name):
    ntok, d = h.shape
    kdim = a.shape[1]
    return pl.pallas_call(
        functools.partial(_proj_res_kernel, n_lat=n_lat),
        out_shape=jax.ShapeDtypeStruct((ntok, d), F32),
        grid=(ntok // BM, d // BN),
        in_specs=[
            pl.BlockSpec((BM, kdim), lambda i, j: (i, 0)),
            _layer_spec((kdim, BN), lambda i, j: (0, j), layer),
            pl.BlockSpec((BM, BN), lambda i, j: (i, j)),
            _layer_spec((8, BN), lambda i, j: (0, gate_col // BN + j), layer),
        ],
        out_specs=pl.BlockSpec((BM, BN), lambda i, j: (i, j)),
        compiler_params=_cparams(("arbitrary", "arbitrary")),
        name=name,
    )(a, w, h, mod)


def _ffn_tile(u_ref, wa_ref, wg_ref, cwa_ref, cwg_ref, cba_ref, cbg_ref, o_ref, ya_ref, yg_ref,
              row0, n_lat, n_tok):
    bm = o_ref.shape[0]
    rows = row0 + lax.broadcasted_iota(jnp.int32, (bm, 1), 0)
    has_prev = ((rows != 0) & (rows != n_lat)).astype(F32)
    has_next = ((rows != n_lat - 1) & (rows != n_tok - 1)).astype(F32)

    def conv(y_ref, cw_ref, cb_ref, cols):
        prev = y_ref[pl.ds(HALO - 1, bm), cols] * has_prev
        cur = y_ref[pl.ds(HALO, bm), cols]
        nxt = y_ref[pl.ds(HALO + 1, bm), cols] * has_next
        return cb_ref[:, cols] + prev * cw_ref[0:1, cols] + cur * cw_ref[1:2, cols] + nxt * cw_ref[2:3, cols]

    u = u_ref[...]
    for s in range(o_ref.shape[1] // FFN_SUB):
        cols = slice(s * FFN_SUB, (s + 1) * FFN_SUB)
        ya_ref[:, cols] = _dot(u, wa_ref[:, cols])
        yg_ref[:, cols] = _dot(u, wg_ref[:, cols])
        a = conv(ya_ref, cwa_ref, cba_ref, cols)
        gt = conv(yg_ref, cwg_ref, cbg_ref, cols)
        o_ref[:, cols] = (gt * _sigmoid(gt) * a).astype(BF16)


def _ffn_up_kernel(h_ref, hp_ref, hn_ref, mod_ref, g_ref, wa_ref, wg_ref, cwa_ref, cwg_ref,
                   cba_ref, cbg_ref, o_ref, u_ref, ya_ref, yg_ref, *, n_lat, n_tok):
    i, j = pl.program_id(0), pl.program_id(1)
    bm = h_ref.shape[0]
    col = 3 * h_ref.shape[1]
    row0 = i * bm

    @pl.when(j == 0)
    def _():
        _norm_modulate(hp_ref, u_ref, 0, g_ref, mod_ref, col, row0 - HALO, n_lat)
        _norm_modulate(h_ref, u_ref, HALO, g_ref, mod_ref, col, row0, n_lat)
        _norm_modulate(hn_ref, u_ref, HALO + bm, g_ref, mod_ref, col, row0 + bm, n_lat)

    _ffn_tile(u_ref, wa_ref, wg_ref, cwa_ref, cwg_ref, cba_ref, cbg_ref, o_ref, ya_ref, yg_ref,
              row0, n_lat, n_tok)


def _ffn_up(h, mod, g, wa, wg, cwa, cwg, cba, cbg, layer, n_lat):
    ntok, d = h.shape
    ff = wa.shape[-1]
    per = BM // HALO
    last = ntok // HALO - 1
    return pl.pallas_call(
        functools.partial(_ffn_up_kernel, n_lat=n_lat, n_tok=ntok),
        out_shape=jax.ShapeDtypeStruct((ntok, ff), BF16),
        grid=(ntok // BM, ff // BN),
        in_specs=[
            pl.BlockSpec((BM, d), lambda i, j: (i, 0)),
            pl.BlockSpec((HALO, d), lambda i, j: (jnp.maximum(i * per - 1, 0), 0)),
            pl.BlockSpec((HALO, d), lambda i, j: (jnp.minimum((i + 1) * per, last), 0)),
            _layer_spec(mod.shape[1:], lambda i, j: (0, 0), layer),
            _layer_spec((1, d), lambda i, j: (0, 0), layer),
            _layer_spec((d, BN), lambda i, j: (0, j), layer),
            _layer_spec((d, BN), lambda i, j: (0, j), layer),
            _layer_spec((3, BN), lambda i, j: (0, j), layer),
            _layer_spec((3, BN), lambda i, j: (0, j), layer),
            _layer_spec((1, BN), lambda i, j: (0, j), layer),
            _layer_spec((1, BN), lambda i, j: (0, j), layer),
        ],
        out_specs=pl.BlockSpec((BM, BN), lambda i, j: (i, j)),
        scratch_shapes=[
            pltpu.VMEM((BM + 2 * HALO, d), BF16),
            pltpu.VMEM((BM + 2 * HALO, BN), F32),
            pltpu.VMEM((BM + 2 * HALO, BN), F32),
        ],
        compiler_params=_cparams(("arbitrary", "arbitrary")),
        name="ffn_up",
    )(h, h, h, mod, g, wa, wg, cwa, cwg, cba, cbg)


def _cast_split_pad_kernel(w_ref, a_ref, g_ref, *, width):
    pad = a_ref.shape[1] - width
    zeros = jnp.zeros((a_ref.shape[0], pad), BF16)
    a_ref[:, :width] = w_ref[:, :width].astype(BF16)
    a_ref[:, width:] = zeros
    g_ref[:, :width] = w_ref[:, width:].astype(BF16)
    g_ref[:, width:] = zeros


def _cast_split_pad(w, width, padded):
    depth, d, _ = w.shape
    rows = 256
    assert width % HEAD == 0 and padded % HEAD == 0 and d % rows == 0
    out = jax.ShapeDtypeStruct((depth, d, padded), BF16)
    return pl.pallas_call(
        functools.partial(_cast_split_pad_kernel, width=width),
        out_shape=(out, out),
        grid=(depth, d // rows),
        in_specs=[pl.BlockSpec((None, rows, 2 * width), lambda l, i: (l, i, 0))],
        out_specs=(pl.BlockSpec((None, rows, padded), lambda l, i: (l, i, 0)),
                   pl.BlockSpec((None, rows, padded), lambda l, i: (l, i, 0))),
        compiler_params=_cparams(("arbitrary", "arbitrary")),
        name="cast_split_pad",
    )(w)


def _rot_layout_a(v):
    sh = v.shape
    return jnp.swapaxes(v.reshape(sh[:-1] + (-1, 2, 2, HEAD // 4)), -3, -2).reshape(sh)


def _rot_layout_c(v):
    sh = v.shape
    return jnp.moveaxis(v.reshape(sh[:-1] + (-1, 2, 2, 2, C_HALF // 4)), -2, -4).reshape(sh)


def _rope_tables(n_lat, n_ctx, dim):
    n_freq = dim // 4
    inv_freq = ROPE_THETA ** (-jnp.arange(n_freq, dtype=F32) / n_freq)
    t = jnp.arange(n_lat, dtype=jnp.int32)
    pos = jnp.stack([t // GRID_W, t % GRID_W], axis=-1).astype(F32)
    ang = (pos[:, :, None] * inv_freq).reshape(n_lat, 2 * n_freq)
    reps = HALF // (2 * n_freq)
    cos_h = jnp.tile(jnp.cos(ang), (1, reps))
    sin_h = jnp.tile(jnp.sin(ang), (1, reps))
    cos_l = jnp.concatenate([cos_h, cos_h], axis=-1)
    sin_l = jnp.concatenate([-sin_h, sin_h], axis=-1)
    cos_l = jnp.concatenate([cos_l, jnp.ones((n_ctx, HEAD), F32)], axis=0)
    sin_l = jnp.concatenate([sin_l, jnp.zeros((n_ctx, HEAD), F32)], axis=0)
    return cos_l, sin_l


def kernel(x, c, ctx, c_ctx, w_mod, b_mod, g_norm1, w_in, gq_a, gk_a, gq_b, gk_b, rpb_b, gq_c, gk_c, lam_q1, lam_k1, lam_q2, lam_k2, g_subln_c, w_br_a, w_br_b, w_br_c, w_o, g_norm2, w_up, conv_w, conv_b, w_down):
    batch, n_lat, d = x.shape
    n_ctx = ctx.shape[1]
    depth = w_mod.shape[0]
    d_ff = w_down.shape[1]
    assert batch == 1 and d == 16 * HEAD
    assert (n_lat + n_ctx) % BM == 0 and n_lat % BQ == 0 and n_ctx == BQ
    ff_pad = -(-d_ff // BN) * BN

    col = lambda a, b: w_in[:, :, a * HEAD:b * HEAD].astype(BF16)
    wqkv = jnp.concatenate([
        _rot_layout_a(col(QA0, KA0)), _rot_layout_a(col(KA0, VA0)), col(VA0, QC0),
        _rot_layout_c(col(QC0, KC0)), _rot_layout_c(col(KC0, VC0)), col(VC0, VC0 + C_HEADS)], axis=-1)
    wbra, wbrb, wbrc, wo = (w.astype(BF16) for w in (w_br_a, w_br_b, w_br_c, w_o))
    padc = ((0, 0), (0, 0), (0, ff_pad - d_ff))
    wupa, wupg = _cast_split_pad(w_up, d_ff, ff_pad)
    cwa = jnp.pad(conv_w[:, :, :d_ff], padc)
    cwg = jnp.pad(conv_w[:, :, d_ff:], padc)
    cba = jnp.pad(conv_b[:, None, :d_ff], padc)
    cbg = jnp.pad(conv_b[:, None, d_ff:], padc)
    wdown = jnp.concatenate([w_down.astype(BF16), jnp.zeros((depth, ff_pad - d_ff, d), BF16)], axis=1)

    ones = jnp.ones((depth, HEAD), F32)
    gains = jnp.stack([_rot_layout_a(gq_a), _rot_layout_a(gk_a), gq_b, gk_b,
                       _rot_layout_c(jnp.tile(gq_c, (1, 2))), _rot_layout_c(jnp.tile(gk_c, (1, 2))),
                       ones, ones], axis=1)
    pad64 = lambda v: jnp.pad(v, ((0, 0), (0, HEAD - C_HALF)))
    lam_init = jnp.asarray([0.8 - 0.6 * math.exp(-0.3 * l) for l in range(depth)], F32)
    cpar = jnp.stack([pad64(lam_q1), pad64(lam_k1), pad64(lam_q2), pad64(lam_k2), g_subln_c,
                      jnp.broadcast_to(lam_init[:, None], (depth, HEAD)), ones, ones], axis=1)
    g1 = g_norm1[:, None, :]
    g2 = g_norm2[:, None, :]

    rope = _rope_tables(n_lat, n_ctx, HEAD) + _rope_tables(n_lat, n_ctx, C_HALF)
    cc = jnp.zeros((8, d), F32).at[0].set(c[0]).at[1].set(c_ctx)

    mod = _modulation(cc, w_mod, b_mod)
    bias = _nbr_bias_tiles(_rpb_blocks(rpb_b), n_lat // GRID_W)

    h = jnp.concatenate([x[0], ctx[0]], axis=0)
    att = dict(n_lat=n_lat, n_ctx=n_ctx)
    for l in range(depth):
        qkv, u = _qkv_proj(h, mod, g1, wqkv, gains, rope, l, n_lat)
        oa = _flash(qkv, cpar, l, n_kv_heads=A_KV, heads=1, groups=A_HEADS // A_KV, diff=False,
                    bq=BQ, bk=min(BK_GQA, n_lat), q0=QA0, k0=KA0, v0=VA0, **att)
        ob = _nbr_attention(qkv, bias, l, **att)
        oc = _flash(qkv, cpar, l, n_kv_heads=C_HEADS, heads=1, groups=2, diff=True,
                    bq=BQ, bk=min(BK_DIFF, n_lat), q0=QC0, k0=KC0, v0=VC0, **att)
        merged = _merge(u, oa, ob, oc, w_in, wbra, wbrb, wbrc, l)
        h = _proj_res(merged, wo, h, mod, 2 * d, l, n_lat, "out_proj")
        act = _ffn_up(h, mod, g2, wupa, wupg, cwa, cwg, cba, cbg, l, n_lat)
        h = _proj_res(act, wdown, h, mod, 5 * d, l, n_lat, "ffn_down")
    return h[:n_lat][None]
```

```python
import functools
import math

import numpy as np
import jax
import jax.numpy as jnp
from jax import lax
from jax.experimental import pallas as pl
from jax.experimental.pallas import tpu as pltpu

F32 = jnp.float32
BF16 = jnp.bfloat16

HEAD = 128
HALF = HEAD // 2
GRID_W = 64
A_HEADS, A_KV = 8, 2
B_HEADS = 4
C_HEADS = 4
C_HALF = 64
NA_KH, NA_KW = 8, 16
ROPE_THETA = 10000.0
EPS = 1e-6
NEG = -1e30
LOG2E = math.log2(math.e)
Q_SCALE_AB = HEAD ** -0.5 * LOG2E
Q_SCALE_C = C_HALF ** -0.5 * LOG2E

QA0, KA0, VA0 = 0, 8, 10
QB0, KB0, VB0 = 12, 16, 20
QC0, KC0, VC0 = 24, 28, 32
N_QKV = 36 * HEAD

VMEM_LIMIT = 56 * 1024 * 1024
MXU_DIM = 256
BM = 768
BM_OUT = 1408
BN = 512
BN_MERGE = 512
FFN_SUB = 256
HALO = 16
NORM_ROWS = 16
BQ = 256
BK_GQA = 4096
BK_DIFF = 2048
NBR_HEADS = 2
NBR_QROWS = BQ // GRID_W
NBR_KROWS = NBR_QROWS + NA_KH
FAST_BOUND = 48.0


def _cparams(sem):
    return pltpu.CompilerParams(dimension_semantics=sem, vmem_limit_bytes=VMEM_LIMIT)


def _sigmoid(x):
    return 1.0 / (1.0 + jnp.exp(-x))


def _dot(a, b):
    return jnp.dot(a, b, preferred_element_type=F32)


def _dot_nt(a, b):
    return lax.dot_general(a, b, (((1,), (1,)), ((), ())), preferred_element_type=F32)


def _layer_spec(block, index_map, layer):
    return pl.BlockSpec((None,) + tuple(block), lambda *g: (layer,) + tuple(index_map(*g)))


def _mod_kernel(cc_ref, w_ref, b_ref, o_ref):
    cc = cc_ref[...]
    a = (cc * _sigmoid(cc)).astype(BF16)
    o_ref[...] = _dot(a, w_ref[...].astype(BF16)) + b_ref[...]


def _modulation(cc, w_mod, b_mod):
    depth, d, n = w_mod.shape
    bn = 1024
    return pl.pallas_call(
        _mod_kernel,
        out_shape=jax.ShapeDtypeStruct((depth, 8, n), F32),
        grid=(depth, n // bn),
        in_specs=[
            pl.BlockSpec((8, d), lambda l, j: (0, 0)),
            pl.BlockSpec((None, d, bn), lambda l, j: (l, 0, j)),
            pl.BlockSpec((None, 1, bn), lambda l, j: (l, 0, j)),
        ],
        out_specs=pl.BlockSpec((None, 8, bn), lambda l, j: (l, 0, j)),
        compiler_params=_cparams(("arbitrary", "arbitrary")),
        name="adaln_mod",
    )(cc, w_mod, b_mod.reshape(depth, 1, n))


def _norm_modulate(h_ref, u_ref, u_row0, g_ref, mod_ref, col0, row0, n_lat):
    n, d = h_ref.shape
    g = g_ref[...]

    def body(c, carry):
        r = pl.multiple_of(c * NORM_ROWS, NORM_ROWS)
        x = h_ref[pl.ds(r, NORM_ROWS), :]
        y = x * lax.rsqrt(jnp.mean(x * x, axis=-1, keepdims=True) + EPS) * g
        who = (row0 + r >= n_lat).astype(jnp.int32)
        shift = mod_ref[pl.ds(who, 1), col0:col0 + d]
        scale = mod_ref[pl.ds(who, 1), col0 + d:col0 + 2 * d]
        dst = pl.multiple_of(u_row0 + r, NORM_ROWS)
        u_ref[pl.ds(dst, NORM_ROWS), :] = (y * (1.0 + scale) + shift).astype(BF16)
        return carry

    trips = n // NORM_ROWS
    lax.fori_loop(0, trips, body, 0, unroll=math.gcd(trips, 4))


def _diff_lanes(shape):
    return (lax.broadcasted_iota(jnp.int32, shape, 1) % HALF) < HALF // 2


def _rms_heads(x, gain, two_components):
    x2 = x * x
    if not two_components:
        ms = jnp.mean(x2, axis=-1, keepdims=True)
    else:
        first = _diff_lanes(x.shape)
        s0 = jnp.sum(jnp.where(first, x2, 0.0), axis=-1, keepdims=True)
        s1 = jnp.sum(jnp.where(first, 0.0, x2), axis=-1, keepdims=True)
        ms = jnp.where(first, s0, s1) * (1.0 / C_HALF)
    return x * lax.rsqrt(ms + EPS) * gain


def _rope(x, cos, sin_signed):
    return x * cos + pltpu.roll(x, HALF, 1) * sin_signed


_RAW = ("raw", 0, 1.0)
_QKV_BLOCKS = (
    (("rope_a", 0, Q_SCALE_AB),) * 4, (("rope_a", 0, Q_SCALE_AB),) * 4,
    (("rope_a", 1, 1.0), ("rope_a", 1, 1.0), _RAW, _RAW),
    (("norm", 2, Q_SCALE_AB),) * 4, (("norm", 3, 1.0),) * 4, (_RAW,) * 4,
    (("rope_c", 4, Q_SCALE_C),) * 4, (("rope_c", 5, 1.0),) * 4, (_RAW,) * 4,
)


def _qkv_kernel(h_ref, mod_ref, g_ref, w_ref, gains_ref, ca_ref, sa_ref, cc_ref, sc_ref,
                o_ref, u_ref, *, n_lat):
    i, j = pl.program_id(0), pl.program_id(1)
    bm = h_ref.shape[0]

    @pl.when(j == 0)
    def _():
        _norm_modulate(h_ref, u_ref, 0, g_ref, mod_ref, 0, i * bm, n_lat)

    acc = _dot(u_ref[...], w_ref[...])

    def epilogue(heads):
        for k, (kind, gr, out_scale) in enumerate(heads):
            x = acc[:, k * HEAD:(k + 1) * HEAD]
            gain = gains_ref[gr:gr + 1, :] * out_scale
            if kind == "rope_a":
                x = _rope(_rms_heads(x, gain, False), ca_ref[...], sa_ref[...])
            elif kind == "norm":
                x = _rms_heads(x, gain, False)
            elif kind == "rope_c":
                x = _rope(_rms_heads(x, gain, True), cc_ref[...], sc_ref[...])
            o_ref[:, k * HEAD:(k + 1) * HEAD] = x.astype(BF16)

    for heads in sorted(set(_QKV_BLOCKS)):
        cond = None
        for jj, hh in enumerate(_QKV_BLOCKS):
            if hh == heads:
                cond = (j == jj) if cond is None else (cond | (j == jj))
        pl.when(cond)(functools.partial(epilogue, heads))


def _qkv_proj(h, mod, g, w, gains, rope, layer, n_lat):
    ntok, d = h.shape
    n = w.shape[-1]
    ca, sa, cc, sc = rope
    return pl.pallas_call(
        functools.partial(_qkv_kernel, n_lat=n_lat),
        out_shape=(jax.ShapeDtypeStruct((ntok, n), BF16), jax.ShapeDtypeStruct((ntok, d), BF16)),
        grid=(ntok // BM, n // BN),
        in_specs=[
            pl.BlockSpec((BM, d), lambda i, j: (i, 0)),
            _layer_spec(mod.shape[1:], lambda i, j: (0, 0), layer),
            _layer_spec((1, d), lambda i, j: (0, 0), layer),
            _layer_spec((d, BN), lambda i, j: (0, j), layer),
            _layer_spec(gains.shape[1:], lambda i, j: (0, 0), layer),
            pl.BlockSpec((BM, HEAD), lambda i, j: (i, 0)),
            pl.BlockSpec((BM, HEAD), lambda i, j: (i, 0)),
            pl.BlockSpec((BM, HEAD), lambda i, j: (i, 0)),
            pl.BlockSpec((BM, HEAD), lambda i, j: (i, 0)),
        ],
        out_specs=(pl.BlockSpec((BM, BN), lambda i, j: (i, j)),
                   pl.BlockSpec((BM, d), lambda i, j: (i, 0))),
        compiler_params=_cparams(("arbitrary", "arbitrary")),
        name="qkv_proj",
    )(h, mod, g, w, gains, ca, sa, cc, sc)


def _flash_kernel(q_ref, k_ref, v_ref, par_ref, o_ref,
                  ka_ref, va_ref, kmax_ref, qa_ref, acc_ref, m_ref, l_ref, acc2_ref, p_ref, lsum_ref,
                  *, heads, groups, diff, bk, n_lat, n_ctx):
    i = pl.program_id(1)
    bq = q_ref.shape[0]
    ntok = k_ref.shape[0]
    qw = q_ref.shape[1] // heads
    is_lat = i * bq < n_lat
    n_main = jnp.where(is_lat, n_lat // bk - 1, 0)
    last0 = n_lat - bk

    def lane_sets(shape):
        if not diff:
            return [None] * groups
        first = _diff_lanes(shape)
        return [first, jnp.logical_not(first)]

    @pl.when(i == 0)
    def _():
        unit = (lax.broadcasted_iota(jnp.int32, (ntok, HEAD), 1) == 0).astype(F32).astype(BF16)
        for h in range(heads):
            k = k_ref[:, h * HEAD:(h + 1) * HEAD]
            ka_ref[h, :, :HEAD] = k
            ka_ref[h, :, HEAD:] = unit
            if not diff:
                va_ref[h, :, :HEAD] = v_ref[:, h * HEAD:(h + 1) * HEAD]
                va_ref[h, :, HEAD:] = unit
            k2 = k.astype(F32)
            k2 = k2 * k2
            for g, lanes in enumerate(lane_sets(k2.shape)[:2 if diff else 1]):
                kk = k2 if lanes is None else jnp.where(lanes, k2, 0.0)
                kmax_ref[2 * h + g] = jnp.max(jnp.sum(kk, axis=-1, keepdims=True), axis=0, keepdims=True)

    lane0 = lax.broadcasted_iota(jnp.int32, (bq, HEAD), 1) == 0
    bound = None
    for h in range(heads):
        for g, lanes in enumerate(lane_sets((bq, HEAD))):
            if diff:
                qv = q_ref[:, h * qw:(h + 1) * qw]
                qg = jnp.where(lanes, qv, jnp.zeros_like(qv))
            else:
                qg = q_ref[:, h * qw + g * HEAD:h * qw + (g + 1) * HEAD]
            qf = qg.astype(F32)
            b = jnp.sqrt(jnp.sum(qf * qf, axis=-1, keepdims=True) * kmax_ref[2 * h + (g if diff else 0)])
            qa_ref[h, g * bq:(g + 1) * bq, :HEAD] = qg
            qa_ref[h, g * bq:(g + 1) * bq, HEAD:] = jnp.where(lane0, -b, 0.0).astype(BF16)
            bound = b if bound is None else jnp.maximum(bound, b)
    fast = jnp.max(bound) <= FAST_BOUND

    if diff:
        lam = (jnp.exp(jnp.sum(par_ref[0:1, :] * par_ref[1:2, :], axis=-1, keepdims=True))
               - jnp.exp(jnp.sum(par_ref[2:3, :] * par_ref[3:4, :], axis=-1, keepdims=True))
               + par_ref[5:6, 0:1])

    def finish_diff(h, o):
        o = o * lax.rsqrt(jnp.mean(o * o, axis=-1, keepdims=True) + EPS) * par_ref[4:5, :]
        o_ref[:, h * qw:(h + 1) * qw] = (o * (1.0 - par_ref[5:6, 0:1])).astype(BF16)

    def finalize(get):
        for h in range(heads):
            if diff:
                (a0, l0), (a1, l1) = get(h, 0), get(h, 1)
                finish_diff(h, a0 / l0 - lam * (a1 / l1))
            else:
                for g in range(groups):
                    a, l = get(h, g)
                    o_ref[:, h * qw + g * HEAD:h * qw + (g + 1) * HEAD] = (a / l).astype(BF16)

    def sweep(chunk, kr, vr):
        def body(c, carry):
            sl = pl.ds(pl.multiple_of(c * bk, bk), bk)
            chunk(lambda h: kr(h, sl), lambda h: vr(h, sl))
            return carry

        lax.fori_loop(0, n_main, body, 0)

        @pl.when(is_lat)
        def _():
            sl = slice(last0, n_lat + n_ctx)
            chunk(lambda h: kr(h, sl), lambda h: vr(h, sl))

        @pl.when(jnp.logical_not(is_lat))
        def _():
            sl = slice(n_lat, n_lat + n_ctx)
            chunk(lambda h: kr(h, sl), lambda h: vr(h, sl))

    def fast_gqa():
        acc_ref[...] = jnp.zeros(acc_ref.shape, F32)

        def chunk(k, v):
            for h in range(heads):
                p = jnp.exp2(_dot_nt(qa_ref[h], k(h))).astype(BF16)
                acc_ref[h] += _dot(p, v(h))

        sweep(chunk, lambda h, sl: ka_ref[h, sl, :], lambda h, sl: va_ref[h, sl, :])
        finalize(lambda h, g: (acc_ref[h, g * bq:(g + 1) * bq, :HEAD],
                               acc_ref[h, g * bq:(g + 1) * bq, HEAD:HEAD + 1]))

    def fast_diff():
        lat = [slice(c * bk, (c + 1) * bk) for c in range(n_lat // bk)]
        ctx = slice(n_lat, n_lat + n_ctx)
        out = acc_ref.at[0, 0:bq, 0:HEAD]
        lsum_ref[...] = jnp.zeros(lsum_ref.shape, F32)

        def pass1(sl):
            p = jnp.exp2(_dot_nt(qa_ref[0], ka_ref[0, sl, :]))
            p_ref[:, sl] = p
            part = p[:, 0:HEAD]
            for t in range(1, (sl.stop - sl.start) // HEAD):
                part = part + p[:, t * HEAD:(t + 1) * HEAD]
            lsum_ref[...] += part

        @pl.when(is_lat)
        def _():
            for sl in lat:
                pass1(sl)

        pass1(ctx)
        l = jnp.sum(lsum_ref[...], axis=-1, keepdims=True)
        r0 = 1.0 / l[0:bq]
        r1 = lam / l[bq:2 * bq]
        out[...] = jnp.zeros((bq, HEAD), F32)

        def pass2(sl):
            a = (p_ref[0:bq, sl] * r0 - p_ref[bq:2 * bq, sl] * r1).astype(BF16)
            out[...] += _dot(a, v_ref[sl, :])

        @pl.when(is_lat)
        def _():
            for sl in lat:
                pass2(sl)

        pass2(ctx)
        finish_diff(0, out[...])

    pl.when(fast)(fast_diff if diff else fast_gqa)

    @pl.when(jnp.logical_not(fast))
    def _():
        m_ref[...] = jnp.full(m_ref.shape, NEG, F32)
        l_ref[...] = jnp.zeros(l_ref.shape, F32)
        acc2_ref[...] = jnp.zeros(acc2_ref.shape, F32)

        def chunk(k, v):
            for h in range(heads):
                for g in range(groups):
                    r = h * groups + g
                    s = _dot_nt(qa_ref[h, g * bq:(g + 1) * bq, :HEAD], k(h))
                    m_prev = m_ref[r]
                    m_new = jnp.maximum(m_prev, jnp.max(s, axis=-1, keepdims=True))
                    alpha = jnp.exp2(m_prev - m_new)
                    p = jnp.exp2(s - m_new)
                    l_ref[r] = alpha * l_ref[r] + jnp.sum(p, axis=-1, keepdims=True)
                    acc2_ref[r] = alpha * acc2_ref[r] + _dot(p.astype(BF16), v(h))
                    m_ref[r] = m_new

        sweep(chunk, lambda h, sl: k_ref[sl, h * HEAD:(h + 1) * HEAD],
              lambda h, sl: v_ref[sl, h * HEAD:(h + 1) * HEAD])
        finalize(lambda h, g: (acc2_ref[h * groups + g], l_ref[h * groups + g]))


def _flash(qkv, par, layer, *, n_kv_heads, heads, groups, diff, bq, bk, q0, k0, v0, n_lat, n_ctx):
    ntok = qkv.shape[0]
    qw = heads * (HEAD if diff else groups * HEAD)
    kw = heads * HEAD
    assert n_lat % bq == 0 and n_ctx % bq == 0 and n_lat % bk == 0 and (heads == 1 or not diff)
    return pl.pallas_call(
        functools.partial(_flash_kernel, heads=heads, groups=groups, diff=diff, bk=bk,
                          n_lat=n_lat, n_ctx=n_ctx),
        out_shape=jax.ShapeDtypeStruct((ntok, n_kv_heads // heads * qw), BF16),
        grid=(n_kv_heads // heads, ntok // bq),
        in_specs=[
            pl.BlockSpec((bq, qw), lambda h, i: (i, q0 * HEAD // qw + h)),
            pl.BlockSpec((ntok, kw), lambda h, i: (0, k0 * HEAD // kw + h)),
            pl.BlockSpec((ntok, kw), lambda h, i: (0, v0 * HEAD // kw + h)),
            _layer_spec(par.shape[1:], lambda h, i: (0, 0), layer),
        ],
        out_specs=pl.BlockSpec((bq, qw), lambda h, i: (i, h)),
        scratch_shapes=[
            pltpu.VMEM((heads, ntok, MXU_DIM), BF16),
            pltpu.VMEM((heads, HALO if diff else ntok, MXU_DIM), BF16),
            pltpu.VMEM((2 * heads, 1, 1), F32),
            pltpu.VMEM((heads, groups * bq, MXU_DIM), BF16),
            pltpu.VMEM((heads, groups * bq, MXU_DIM), F32),
            pltpu.VMEM((heads * groups, bq, 1), F32),
            pltpu.VMEM((heads * groups, bq, 1), F32),
            pltpu.VMEM((heads * groups, bq, HEAD), F32),
            pltpu.VMEM((groups * bq, ntok) if diff else (8, HEAD), F32),
            pltpu.VMEM((groups * bq, HEAD) if diff else (8, HEAD), F32),
        ],
        compiler_params=_cparams(("arbitrary", "arbitrary")),
        name="flash_diff" if diff else "flash_gqa",
    )(qkv, qkv, qkv, par)


def _rpb_blocks_kernel(rpb_ref, o_ref):
    n_dr, n_dc = 2 * NA_KH - 1, 2 * NA_KW - 1
    base = pl.program_id(0) * (n_dr * n_dc)
    qc = lax.broadcasted_iota(jnp.int32, (GRID_W, GRID_W), 0)
    kc = lax.broadcasted_iota(jnp.int32, (GRID_W, GRID_W), 1)
    cs = jnp.clip(qc - NA_KW // 2, 0, GRID_W - NA_KW)
    in_win = (kc >= cs) & (kc < cs + NA_KW)
    dc = kc - qc + (NA_KW - 1)
    for a in range(n_dr):
        blk = jnp.zeros((GRID_W, GRID_W), F32)
        for b in range(n_dc):
            blk = jnp.where(dc == b, rpb_ref[base + a * n_dc + b], blk)
        o_ref[a] = jnp.where(in_win, blk * LOG2E, NEG)


def _rpb_blocks(rpb):
    lh = rpb.shape[0] * rpb.shape[1]
    n_dr = 2 * NA_KH - 1
    return pl.pallas_call(
        _rpb_blocks_kernel,
        out_shape=jax.ShapeDtypeStruct((lh, n_dr, GRID_W, GRID_W), F32),
        grid=(lh,),
        in_specs=[pl.BlockSpec(memory_space=pltpu.SMEM)],
        out_specs=pl.BlockSpec((None, n_dr, GRID_W, GRID_W), lambda p: (p, 0, 0, 0)),
        compiler_params=_cparams(("arbitrary",)),
        name="rpb_blocks",
    )(rpb.reshape(-1))


def _nbr_window_plan(rows):
    nblk = rows // NBR_QROWS

    def plan(blk):
        r0 = blk * NBR_QROWS
        k0 = min(max(r0 - NA_KH // 2, 0), rows - NBR_KROWS)
        dr = np.zeros((NBR_QROWS, NBR_KROWS), np.int32)
        ok = np.zeros((NBR_QROWS, NBR_KROWS), bool)
        for qr in range(NBR_QROWS):
            r = r0 + qr
            rs = min(max(r - NA_KH // 2, 0), rows - NA_KH)
            for kr in range(NBR_KROWS):
                ok[qr, kr] = rs <= k0 + kr < rs + NA_KH
                dr[qr, kr] = min(max(k0 + kr - r + NA_KH - 1, 0), 2 * NA_KH - 2)
        return dr, ok

    classes = [plan(0), plan(1), plan(nblk - 1)]
    for blk in range(1, nblk - 1):
        dr, ok = plan(blk)
        assert (dr[ok] == classes[1][0][ok]).all() and (ok == classes[1][1]).all()
    return np.stack([c[0] for c in classes]), np.stack([c[1] for c in classes])


def _nbr_bias_tiles(blocks, rows):
    dr, ok = _nbr_window_plan(rows)
    masked = jnp.full((blocks.shape[0], GRID_W, GRID_W), NEG, F32)
    classes = []
    for c in range(dr.shape[0]):
        strips = [jnp.concatenate([blocks[:, dr[c, qr, kr]] if ok[c, qr, kr] else masked
                                   for kr in range(NBR_KROWS)], axis=-1) for qr in range(NBR_QROWS)]
        classes.append(jnp.concatenate(strips, axis=-2))
    return jnp.stack(classes, axis=1)


def _nbr_kernel(q_ref, k_ref, v_ref, bias_ref, o_ref, *, n_lat, n_ctx):
    i = pl.program_id(1)
    bq = q_ref.shape[0]
    rows = n_lat // GRID_W
    head = lambda ref, h, sl: ref[sl, h * HEAD:(h + 1) * HEAD]
    ctx = slice(n_lat, n_lat + n_ctx)

    @pl.when(i * bq < n_lat)
    def _():
        k0 = jnp.clip(i * NBR_QROWS - NA_KH // 2, 0, rows - NBR_KROWS)
        win = pl.ds(pl.multiple_of(k0 * GRID_W, NBR_QROWS * GRID_W), NBR_KROWS * GRID_W)
        for h in range(NBR_HEADS):
            q = head(q_ref, h, slice(None))
            s_ctx = _dot_nt(q, head(k_ref, h, ctx))
            s_win = _dot_nt(q, head(k_ref, h, win)) + bias_ref[h]
            m = jnp.maximum(jnp.max(s_ctx, axis=-1, keepdims=True), jnp.max(s_win, axis=-1, keepdims=True))
            p_win = jnp.exp2(s_win - m)
            p_ctx = jnp.exp2(s_ctx - m)
            l = jnp.sum(p_win, axis=-1, keepdims=True) + jnp.sum(p_ctx, axis=-1, keepdims=True)
            o = _dot(p_win.astype(BF16), head(v_ref, h, win)) + _dot(p_ctx.astype(BF16), head(v_ref, h, ctx))
            o_ref[:, h * HEAD:(h + 1) * HEAD] = (o / l).astype(BF16)

    @pl.when(i * bq >= n_lat)
    def _():
        for h in range(NBR_HEADS):
            s_ctx = _dot_nt(head(q_ref, h, slice(None)), head(k_ref, h, ctx))
            p_ctx = jnp.exp2(s_ctx - jnp.max(s_ctx, axis=-1, keepdims=True))
            l = jnp.sum(p_ctx, axis=-1, keepdims=True)
            o_ref[:, h * HEAD:(h + 1) * HEAD] = (_dot(p_ctx.astype(BF16), head(v_ref, h, ctx)) / l).astype(BF16)


def _nbr_attention(qkv, bias, layer, *, n_lat, n_ctx):
    ntok = qkv.shape[0]
    n_lat_blocks = n_lat // BQ
    nk = NBR_KROWS * GRID_W
    w = NBR_HEADS * HEAD
    groups = B_HEADS // NBR_HEADS

    def bias_map(h, i):
        cls = jnp.where(i == 0, 0, jnp.where(i == n_lat_blocks - 1, 2, 1))
        return (layer * groups + h, cls, 0, 0)

    return pl.pallas_call(
        functools.partial(_nbr_kernel, n_lat=n_lat, n_ctx=n_ctx),
        out_shape=jax.ShapeDtypeStruct((ntok, B_HEADS * HEAD), BF16),
        grid=(groups, ntok // BQ),
        in_specs=[
            pl.BlockSpec((BQ, w), lambda h, i: (i, QB0 * HEAD // w + h)),
            pl.BlockSpec((ntok, w), lambda h, i: (0, KB0 * HEAD // w + h)),
            pl.BlockSpec((ntok, w), lambda h, i: (0, VB0 * HEAD // w + h)),
            pl.BlockSpec((NBR_HEADS, None, BQ, nk), bias_map),
        ],
        out_specs=pl.BlockSpec((BQ, w), lambda h, i: (i, h)),
        compiler_params=_cparams(("arbitrary", "arbitrary")),
        name="nbr_attn",
    )(qkv, qkv, qkv, bias)


def _merge_kernel(u_ref, oa_ref, ob_ref, oc_ref, wga_ref, wgb_ref, wgc_ref, wa_ref, wb_ref, wc_ref, o_ref):
    u = u_ref[...]
    merged = _sigmoid(_dot(u, wga_ref[...].astype(BF16))) * _dot(oa_ref[...], wa_ref[...])
    merged += _sigmoid(_dot(u, wgb_ref[...].astype(BF16))) * _dot(ob_ref[...], wb_ref[...])
    merged += _sigmoid(_dot(u, wgc_ref[...].astype(BF16))) * _dot(oc_ref[...], wc_ref[...])
    o_ref[...] = merged.astype(BF16)


def _merge(u, oa, ob, oc, w_in, wa, wb, wc, layer):
    ntok, d = u.shape
    assert N_QKV % BN_MERGE == 0 and d % BN_MERGE == 0
    gate = lambda b: _layer_spec((d, BN_MERGE), lambda i, j: (0, (N_QKV + b * d) // BN_MERGE + j), layer)
    return pl.pallas_call(
        _merge_kernel,
        out_shape=jax.ShapeDtypeStruct((ntok, d), BF16),
        grid=(ntok // BM, d // BN_MERGE),
        in_specs=[
            pl.BlockSpec((BM, d), lambda i, j: (i, 0)),
            pl.BlockSpec((BM, oa.shape[1]), lambda i, j: (i, 0)),
            pl.BlockSpec((BM, ob.shape[1]), lambda i, j: (i, 0)),
            pl.BlockSpec((BM, oc.shape[1]), lambda i, j: (i, 0)),
            gate(0), gate(1), gate(2),
            _layer_spec((wa.shape[1], BN_MERGE), lambda i, j: (0, j), layer),
            _layer_spec((wb.shape[1], BN_MERGE), lambda i, j: (0, j), layer),
            _layer_spec((wc.shape[1], BN_MERGE), lambda i, j: (0, j), layer),
        ],
        out_specs=pl.BlockSpec((BM, BN_MERGE), lambda i, j: (i, j)),
        compiler_params=_cparams(("arbitrary", "arbitrary")),
        name="branch_merge",
    )(u, oa, ob, oc, w_in, w_in, w_in, wa, wb, wc)


def _proj_res_kernel(a_ref, w_ref, h_ref, gate_ref, o_ref, *, n_lat):
    bm = h_ref.shape[0]
    rows = pl.program_id(0) * bm + lax.broadcasted_iota(jnp.int32, (bm, 1), 0)
    gate = jnp.where(rows >= n_lat, gate_ref[1:2, :], gate_ref[0:1, :])
    o_ref[...] = h_ref[...] + gate * _dot(a_ref[...], w_ref[...])


def _proj_res(a, w, h, mod, gate_col, layer, n_lat, bm, name):
    ntok, d = h.shape
    kdim = w.shape[1]
    assert ntok % bm == 0 and kdim % HEAD == 0 and kdim <= a.shape[1]
    return pl.pallas_call(
        functools.partial(_proj_res_kernel, n_lat=n_lat),
        out_shape=jax.ShapeDtypeStruct((ntok, d), F32),
        grid=(ntok // bm, d // BN),
        in_specs=[
            pl.BlockSpec((bm, kdim), lambda i, j: (i, 0)),
            _layer_spec((kdim, BN), lambda i, j: (0, j), layer),
            pl.BlockSpec((bm, BN), lambda i, j: (i, j)),
            _layer_spec((8, BN), lambda i, j: (0, gate_col // BN + j), layer),
        ],
        out_specs=pl.BlockSpec((bm, BN), lambda i, j: (i, j)),
        compiler_params=_cparams(("arbitrary", "arbitrary")),
        name=name,
    )(a, w, h, mod)


def _ffn_tile(u_ref, wa_ref, wg_ref, cwa_ref, cwg_ref, cba_ref, cbg_ref, o_ref, ya_ref, yg_ref,
              row0, n_lat, n_tok):
    bm = o_ref.shape[0]
    rows = row0 + lax.broadcasted_iota(jnp.int32, (bm, 1), 0)
    has_prev = ((rows != 0) & (rows != n_lat)).astype(F32)
    has_next = ((rows != n_lat - 1) & (rows != n_tok - 1)).astype(F32)

    def conv(y_ref, cw_ref, cb_ref, cols):
        prev = y_ref[pl.ds(HALO - 1, bm), cols] * has_prev
        cur = y_ref[pl.ds(HALO, bm), cols]
        nxt = y_ref[pl.ds(HALO + 1, bm), cols] * has_next
        return cb_ref[:, cols] + prev * cw_ref[0:1, cols] + cur * cw_ref[1:2, cols] + nxt * cw_ref[2:3, cols]

    u = u_ref[...]
    for s in range(o_ref.shape[1] // FFN_SUB):
        cols = slice(s * FFN_SUB, (s + 1) * FFN_SUB)
        ya_ref[:, cols] = _dot(u, wa_ref[:, cols])
        yg_ref[:, cols] = _dot(u, wg_ref[:, cols])
        a = conv(ya_ref, cwa_ref, cba_ref, cols)
        gt = conv(yg_ref, cwg_ref, cbg_ref, cols)
        o_ref[:, cols] = (gt * _sigmoid(gt) * a).astype(BF16)


def _ffn_up_kernel(h_ref, hp_ref, hn_ref, mod_ref, g_ref, wa_ref, wg_ref, cwa_ref, cwg_ref,
                   cba_ref, cbg_ref, o_ref, u_ref, ya_ref, yg_ref, *, n_lat, n_tok):
    i, j = pl.program_id(0), pl.program_id(1)
    bm = h_ref.shape[0]
    col = 3 * h_ref.shape[1]
    row0 = i * bm

    @pl.when(j == 0)
    def _():
        _norm_modulate(hp_ref, u_ref, 0, g_ref, mod_ref, col, row0 - HALO, n_lat)
        _norm_modulate(h_ref, u_ref, HALO, g_ref, mod_ref, col, row0, n_lat)
        _norm_modulate(hn_ref, u_ref, HALO + bm, g_ref, mod_ref, col, row0 + bm, n_lat)

    _ffn_tile(u_ref, wa_ref, wg_ref, cwa_ref, cwg_ref, cba_ref, cbg_ref, o_ref, ya_ref, yg_ref,
              row0, n_lat, n_tok)


def _ffn_up(h, mod, g, wa, wg, cwa, cwg, cba, cbg, layer, n_lat):
    ntok, d = h.shape
    ff = wa.shape[-1]
    per = BM // HALO
    last = ntok // HALO - 1
    return pl.pallas_call(
        functools.partial(_ffn_up_kernel, n_lat=n_lat, n_tok=ntok),
        out_shape=jax.ShapeDtypeStruct((ntok, ff), BF16),
        grid=(ntok // BM, ff // BN),
        in_specs=[
            pl.BlockSpec((BM, d), lambda i, j: (i, 0)),
            pl.BlockSpec((HALO, d), lambda i, j: (jnp.maximum(i * per - 1, 0), 0)),
            pl.BlockSpec((HALO, d), lambda i, j: (jnp.minimum((i + 1) * per, last), 0)),
            _layer_spec(mod.shape[1:], lambda i, j: (0, 0), layer),
            _layer_spec((1, d), lambda i, j: (0, 0), layer),
            _layer_spec((d, BN), lambda i, j: (0, j), layer),
            _layer_spec((d, BN), lambda i, j: (0, j), layer),
            _layer_spec((3, BN), lambda i, j: (0, j), layer),
            _layer_spec((3, BN), lambda i, j: (0, j), layer),
            _layer_spec((1, BN), lambda i, j: (0, j), layer),
            _layer_spec((1, BN), lambda i, j: (0, j), layer),
        ],
        out_specs=pl.BlockSpec((BM, BN), lambda i, j: (i, j)),
        scratch_shapes=[
            pltpu.VMEM((BM + 2 * HALO, d), BF16),
            pltpu.VMEM((BM + 2 * HALO, BN), F32),
            pltpu.VMEM((BM + 2 * HALO, BN), F32),
        ],
        compiler_params=_cparams(("arbitrary", "arbitrary")),
        name="ffn_up",
    )(h, h, h, mod, g, wa, wg, cwa, cwg, cba, cbg)


def _cast_split_pad_kernel(w_ref, a_ref, g_ref, *, width):
    pad = a_ref.shape[1] - width
    zeros = jnp.zeros((a_ref.shape[0], pad), BF16)
    a_ref[:, :width] = w_ref[:, :width].astype(BF16)
    a_ref[:, width:] = zeros
    g_ref[:, :width] = w_ref[:, width:].astype(BF16)
    g_ref[:, width:] = zeros


def _cast_split_pad(w, width, padded):
    depth, d, _ = w.shape
    rows = 256
    assert width % HEAD == 0 and padded % HEAD == 0 and d % rows == 0
    out = jax.ShapeDtypeStruct((depth, d, padded), BF16)
    return pl.pallas_call(
        functools.partial(_cast_split_pad_kernel, width=width),
        out_shape=(out, out),
        grid=(depth, d // rows),
        in_specs=[pl.BlockSpec((None, rows, 2 * width), lambda l, i: (l, i, 0))],
        out_specs=(pl.BlockSpec((None, rows, padded), lambda l, i: (l, i, 0)),
                   pl.BlockSpec((None, rows, padded), lambda l, i: (l, i, 0))),
        compiler_params=_cparams(("arbitrary", "arbitrary")),
        name="cast_split_pad",
    )(w)


def _cast_qkv_kernel(w_ref, pa_ref, pc_ref, o_ref):
    j = pl.program_id(1)

    def cast(heads):
        for k, (kind, _, _) in enumerate(heads):
            x = w_ref[:, k * HEAD:(k + 1) * HEAD].astype(BF16)
            if kind == "rope_a":
                x = _dot(x, pa_ref[...]).astype(BF16)
            elif kind == "rope_c":
                x = _dot(x, pc_ref[...]).astype(BF16)
            o_ref[:, k * HEAD:(k + 1) * HEAD] = x

    layouts = [tuple(kind for kind, _, _ in heads) for heads in _QKV_BLOCKS]
    for layout in sorted(set(layouts)):
        cond = None
        for jj, ll in enumerate(layouts):
            if ll == layout:
                cond = (j == jj) if cond is None else (cond | (j == jj))
        pl.when(cond)(functools.partial(cast, _QKV_BLOCKS[layouts.index(layout)]))


def _cast_qkv(w_in):
    depth, d, _ = w_in.shape
    eye = jnp.eye(HEAD, dtype=BF16)
    return pl.pallas_call(
        _cast_qkv_kernel,
        out_shape=jax.ShapeDtypeStruct((depth, d, N_QKV), BF16),
        grid=(depth, N_QKV // BN),
        in_specs=[
            pl.BlockSpec((None, d, BN), lambda l, j: (l, 0, j)),
            pl.BlockSpec((HEAD, HEAD), lambda l, j: (0, 0)),
            pl.BlockSpec((HEAD, HEAD), lambda l, j: (0, 0)),
        ],
        out_specs=pl.BlockSpec((None, d, BN), lambda l, j: (l, 0, j)),
        compiler_params=_cparams(("arbitrary", "arbitrary")),
        name="cast_qkv",
    )(w_in, _rot_layout_a(eye), _rot_layout_c(eye))


def _rot_layout_a(v):
    sh = v.shape
    return jnp.swapaxes(v.reshape(sh[:-1] + (-1, 2, 2, HEAD // 4)), -3, -2).reshape(sh)


def _rot_layout_c(v):
    sh = v.shape
    return jnp.moveaxis(v.reshape(sh[:-1] + (-1, 2, 2, 2, C_HALF // 4)), -2, -4).reshape(sh)


def _rope_tables(n_lat, n_ctx, dim):
    n_freq = dim // 4
    inv_freq = ROPE_THETA ** (-jnp.arange(n_freq, dtype=F32) / n_freq)
    t = jnp.arange(n_lat, dtype=jnp.int32)
    pos = jnp.stack([t // GRID_W, t % GRID_W], axis=-1).astype(F32)
    ang = (pos[:, :, None] * inv_freq).reshape(n_lat, 2 * n_freq)
    reps = HALF // (2 * n_freq)
    cos_h = jnp.tile(jnp.cos(ang), (1, reps))
    sin_h = jnp.tile(jnp.sin(ang), (1, reps))
    cos_l = jnp.concatenate([cos_h, cos_h], axis=-1)
    sin_l = jnp.concatenate([-sin_h, sin_h], axis=-1)
    cos_l = jnp.concatenate([cos_l, jnp.ones((n_ctx, HEAD), F32)], axis=0)
    sin_l = jnp.concatenate([sin_l, jnp.zeros((n_ctx, HEAD), F32)], axis=0)
    return cos_l, sin_l


def kernel(x, c, ctx, c_ctx, w_mod, b_mod, g_norm1, w_in, gq_a, gk_a, gq_b, gk_b, rpb_b, gq_c, gk_c, lam_q1, lam_k1, lam_q2, lam_k2, g_subln_c, w_br_a, w_br_b, w_br_c, w_o, g_norm2, w_up, conv_w, conv_b, w_down):
    batch, n_lat, d = x.shape
    n_ctx = ctx.shape[1]
    depth = w_mod.shape[0]
    d_ff = w_down.shape[1]
    assert batch == 1 and d == 16 * HEAD
    assert (n_lat + n_ctx) % BM == 0 and n_lat % BQ == 0 and n_ctx == BQ
    ff_pad = -(-d_ff // BN) * BN

    wqkv = _cast_qkv(w_in)
    wbra, wbrb, wbrc, wo = (w.astype(BF16) for w in (w_br_a, w_br_b, w_br_c, w_o))
    padc = ((0, 0), (0, 0), (0, ff_pad - d_ff))
    wupa, wupg = _cast_split_pad(w_up, d_ff, ff_pad)
    cwa = jnp.pad(conv_w[:, :, :d_ff], padc)
    cwg = jnp.pad(conv_w[:, :, d_ff:], padc)
    cba = jnp.pad(conv_b[:, None, :d_ff], padc)
    cbg = jnp.pad(conv_b[:, None, d_ff:], padc)
    wdown = w_down.astype(BF16)
    bm_out = BM_OUT if (n_lat + n_ctx) % BM_OUT == 0 else BM

    ones = jnp.ones((depth, HEAD), F32)
    gains = jnp.stack([_rot_layout_a(gq_a), _rot_layout_a(gk_a), gq_b, gk_b,
                       _rot_layout_c(jnp.tile(gq_c, (1, 2))), _rot_layout_c(jnp.tile(gk_c, (1, 2))),
                       ones, ones], axis=1)
    pad64 = lambda v: jnp.pad(v, ((0, 0), (0, HEAD - C_HALF)))
    lam_init = jnp.asarray([0.8 - 0.6 * math.exp(-0.3 * l) for l in range(depth)], F32)
    cpar = jnp.stack([pad64(lam_q1), pad64(lam_k1), pad64(lam_q2), pad64(lam_k2), g_subln_c,
                      jnp.broadcast_to(lam_init[:, None], (depth, HEAD)), ones, ones], axis=1)
    g1 = g_norm1[:, None, :]
    g2 = g_norm2[:, None, :]

    rope = _rope_tables(n_lat, n_ctx, HEAD) + _rope_tables(n_lat, n_ctx, C_HALF)
    cc = jnp.zeros((8, d), F32).at[0].set(c[0]).at[1].set(c_ctx)

    mod = _modulation(cc, w_mod, b_mod)
    bias = _nbr_bias_tiles(_rpb_blocks(rpb_b), n_lat // GRID_W)

    h = jnp.concatenate([x[0], ctx[0]], axis=0)
    att = dict(n_lat=n_lat, n_ctx=n_ctx)
    for l in range(depth):
        qkv, u = _qkv_proj(h, mod, g1, wqkv, gains, rope, l, n_lat)
        oa = _flash(qkv, cpar, l, n_kv_heads=A_KV, heads=1, groups=A_HEADS // A_KV, diff=False,
                    bq=BQ, bk=min(BK_GQA, n_lat), q0=QA0, k0=KA0, v0=VA0, **att)
        ob = _nbr_attention(qkv, bias, l, **att)
        oc = _flash(qkv, cpar, l, n_kv_heads=C_HEADS, heads=1, groups=2, diff=True,
                    bq=BQ, bk=min(BK_DIFF, n_lat), q0=QC0, k0=KC0, v0=VC0, **att)
        merged = _merge(u, oa, ob, oc, w_in, wbra, wbrb, wbrc, l)
        h = _proj_res(merged, wo, h, mod, 2 * d, l, n_lat, bm_out, "out_proj")
        act = _ffn_up(h, mod, g2, wupa, wupg, cwa, cwg, cba, cbg, l, n_lat)
        h = _proj_res(act, wdown, h, mod, 5 * d, l, n_lat, BM, "ffn_down")
    return h[:n_lat][None]
```
